```python
import math
import jax, jax.numpy as jnp
from jax import lax
import numpy as np

D_MODEL = 2048
BATCH = 4
SEQ = 2048
DEPTH = 1
DEC_BATCH = 8
DEC_SEQ = 1
PAST_LEN = 16384
PAGE_SIZE = 128

D_CONV = D_MODEL // 2
CONV_W = 3
N_HEADS = 8
DK = D_MODEL // 32
DV = 2 * DK
QK_W = N_HEADS * 2 * DK
V_W = N_HEADS * DV
ROT_DIM = DK // 4
ROPE_THETA = 500000.0
N_MEM = 256
MEM_HEADS = 4
MEM_HD = D_MODEL // 8
CROSS_W = MEM_HEADS * MEM_HD
N_BRANCH = 3
N_KEYS = 128
N_EXPERTS = N_KEYS * N_KEYS
PEER_HEADS = 8
PEER_TOPK = 16
D_KEY = 256
Q_BLOCK = 128
PEER_BLOCK = 128
EPS = 1e-6
NEG_INF = -1e30
IN_WIDTHS = (D_CONV, D_CONV, D_CONV, QK_W, QK_W, V_W, CROSS_W, N_BRANCH * D_MODEL)
IN_COLS = sum(IN_WIDTHS)
SPLIT_POINTS = tuple(int(s) for s in np.cumsum(IN_WIDTHS)[:-1])

kernel_name = "hybrid_conv_diffattn_mem_peer_step"


def rms_norm(x, w):
    xf = x.astype(jnp.float32)
    y = xf * lax.rsqrt(jnp.mean(xf * xf, axis=-1, keepdims=True) + EPS)
    return (y * w.astype(jnp.float32)).astype(x.dtype)


def partial_rope(x, pos):
    half = ROT_DIM // 2
    inv_freq = ROPE_THETA ** (-jnp.arange(half, dtype=jnp.float32) / half)
    ang = pos.astype(jnp.float32)[:, None] * inv_freq[None, :]
    cos = jnp.cos(ang)[None, :, None, None, :]
    sin = jnp.sin(ang)[None, :, None, None, :]
    xf = x.astype(jnp.float32)
    x1 = xf[..., :half]
    x2 = xf[..., half:ROT_DIM]
    out = jnp.concatenate([x1 * cos - x2 * sin, x2 * cos + x1 * sin, xf[..., ROT_DIM:]], axis=-1)
    return out.astype(x.dtype)


def short_conv(u, buf, conv_w):
    T = u.shape[1]
    ext = jnp.concatenate([buf, u], axis=1)
    y = conv_w[0] * ext[:, 0:T] + conv_w[1] * ext[:, 1:T + 1] + conv_w[2] * ext[:, 2:T + 2]
    return y, ext[:, -(CONV_W - 1):]


def diff_attn(q, k, v, q_pos, k_pos, lam):
    s = jnp.einsum('bqhcd,bkhcd->bhcqk', q, k).astype(jnp.float32) * (DK ** -0.5)
    mask = k_pos[None, :] <= q_pos[:, None]
    s = jnp.where(mask, s, NEG_INF)
    p = jax.nn.softmax(s, axis=-1)
    a = p[:, :, 0] - lam * p[:, :, 1]
    return jnp.einsum('bhqk,bkhd->bqhd', a.astype(v.dtype), v)


def memory_kv(mem, mem_norm_w, w_mem_kv, mk_norm_w):
    B = mem.shape[0]
    kv = rms_norm(mem, mem_norm_w) @ w_mem_kv
    mk, mv = jnp.split(kv, 2, axis=-1)
    mk = rms_norm(mk.reshape(B, N_MEM, MEM_HEADS, MEM_HD), mk_norm_w)
    mv = mv.reshape(B, N_MEM, MEM_HEADS, MEM_HD)
    return mk, mv


def mem_attn(q, mk, mv):
    s = jnp.einsum('bqhd,bmhd->bhqm', q, mk).astype(jnp.float32) * (MEM_HD ** -0.5)
    p = jax.nn.softmax(s, axis=-1)
    return jnp.einsum('bhqm,bmhd->bqhd', p.astype(mv.dtype), mv)


def token_mixer(x, pos, conv_buf, k_past, v_past, mem_k, mem_v, lam, lam_init,
                norm1_w, w_in, conv_w, q_norm_w, k_norm_w, subln_w, mq_norm_w,
                w_conv_out, w_diff_out, w_cross_out, w_o):
    B, T, _ = x.shape
    h = rms_norm(x, norm1_w)
    z = h @ w_in
    zb, zc, zx, zq, zk, zv, zmq, zg = jnp.split(z, SPLIT_POINTS, axis=-1)
    y_c, new_buf = short_conv(zc * zx, conv_buf, conv_w)
    y_conv = (zb * y_c) @ w_conv_out
    q = partial_rope(rms_norm(zq.reshape(B, T, N_HEADS, 2, DK), q_norm_w), pos)
    k = partial_rope(rms_norm(zk.reshape(B, T, N_HEADS, 2, DK), k_norm_w), pos)
    v = zv.reshape(B, T, N_HEADS, DV)
    if k_past is None:
        nb = T // Q_BLOCK
        q_blocks = jnp.moveaxis(q.reshape(B, nb, Q_BLOCK, N_HEADS, 2, DK), 1, 0)
        pos_blocks = pos.reshape(nb, Q_BLOCK)
        o = lax.map(lambda qp: diff_attn(qp[0], k, v, qp[1], pos, lam), (q_blocks, pos_blocks))
        o = jnp.moveaxis(o, 0, 1).reshape(B, T, N_HEADS, DV)
    else:
        k_all = jnp.concatenate([k_past, k], axis=1)
        v_all = jnp.concatenate([v_past, v], axis=1)
        k_pos = jnp.arange(k_all.shape[1], dtype=jnp.int32)
        o = diff_attn(q, k_all, v_all, pos, k_pos, lam)
    o = rms_norm(o, subln_w) * (1.0 - lam_init)
    y_diff = o.reshape(B, T, V_W) @ w_diff_out
    qm = rms_norm(zmq.reshape(B, T, MEM_HEADS, MEM_HD), mq_norm_w)
    y_mem = mem_attn(qm, mem_k, mem_v).reshape(B, T, CROSS_W) @ w_cross_out
    g = jax.nn.sigmoid(zg.reshape(B, T, N_BRANCH, D_MODEL))
    merged = g[:, :, 0] * y_conv + g[:, :, 1] * y_diff + g[:, :, 2] * y_mem
    x = x + merged @ w_o
    return x, k.reshape(B, T, N_HEADS, 2 * DK), v, new_buf


def peer_ffn(h, w_peer_q, sub_keys, peer_u, peer_v):
    B, T, D = h.shape
    n = B * T
    hf = h.reshape(n, D)
    q = (hf @ w_peer_q).reshape(n, PEER_HEADS, 2, D_KEY // 2)
    s1 = jnp.einsum('nhd,kd->nhk', q[:, :, 0], sub_keys[0]).astype(jnp.float32)
    s2 = jnp.einsum('nhd,kd->nhk', q[:, :, 1], sub_keys[1]).astype(jnp.float32)
    v1, i1 = lax.top_k(s1, PEER_TOPK)
    v2, i2 = lax.top_k(s2, PEER_TOPK)
    cand_s = (v1[..., :, None] + v2[..., None, :]).reshape(n, PEER_HEADS, PEER_TOPK * PEER_TOPK)
    cand_i = (i1[..., :, None] * N_KEYS + i2[..., None, :]).reshape(n, PEER_HEADS, PEER_TOPK * PEER_TOPK)
    top_s, top_pos = lax.top_k(cand_s, PEER_TOPK)
    idx = jnp.take_along_axis(cand_i, top_pos, axis=-1)
    gate = jax.nn.softmax(top_s, axis=-1).astype(h.dtype)
    blk = min(PEER_BLOCK, n)
    n_pad = -(-n // blk) * blk
    pad = n_pad - n
    hp = jnp.pad(hf, ((0, pad), (0, 0))).reshape(-1, blk, D)
    ip = jnp.pad(idx, ((0, pad), (0, 0), (0, 0))).reshape(-1, blk, PEER_HEADS, PEER_TOPK)
    gp = jnp.pad(gate, ((0, pad), (0, 0), (0, 0))).reshape(-1, blk, PEER_HEADS, PEER_TOPK)

    def expert_block(args):
        hb, ib, gb = args
        u = jnp.take(peer_u, ib, axis=0)
        a = jnp.einsum('nhkd,nd->nhk', u, hb)
        w = gb * jax.nn.gelu(a, approximate=False)
        vv = jnp.take(peer_v, ib, axis=0)
        return jnp.einsum('nhk,nhkd->nd', w, vv)

    out = lax.map(expert_block, (hp, ip, gp))
    return out.reshape(n_pad, D)[:n].reshape(B, T, D)


def setup_inputs(seed: int = 0) -> dict:
    key = jax.random.key(seed)
    ks = jax.random.split(key, 40)
    f32 = jnp.float32

    def nrm(i, shape, scale):
        return jax.random.normal(ks[i], shape, f32) * scale

    def gain(i, dim):
        return 1.0 + 0.02 * jax.random.normal(ks[i], (DEPTH, dim), f32)

    n_pages = PAST_LEN // PAGE_SIZE
    n_used = DEC_BATCH * n_pages
    n_pool = n_used + (n_used + 3) // 4
    page_table = jax.random.permutation(ks[0], n_pool)[:n_used].reshape(DEC_BATCH, n_pages).astype(jnp.int32)
    return {
        "x_prompt": nrm(1, (BATCH, SEQ, D_MODEL), 1.0),
        "x_sample": nrm(2, (DEC_BATCH, DEC_SEQ, D_MODEL), 1.0),
        "mem_prompt": nrm(3, (BATCH, N_MEM, D_MODEL), 1.0),
        "cache_k": nrm(4, (DEPTH, n_pool, PAGE_SIZE, N_HEADS, 2 * DK), 1.0),
        "cache_v": nrm(5, (DEPTH, n_pool, PAGE_SIZE, N_HEADS, DV), 1.0),
        "page_table": page_table,
        "state_conv": nrm(6, (DEPTH, DEC_BATCH, CONV_W - 1, D_CONV), 1.0),
        "cache_mem_k": nrm(7, (DEPTH, DEC_BATCH, N_MEM, MEM_HEADS, MEM_HD), 1.0),
        "cache_mem_v": nrm(8, (DEPTH, DEC_BATCH, N_MEM, MEM_HEADS, MEM_HD), 1.0),
        "norm1_w": gain(9, D_MODEL),
        "w_in": nrm(10, (DEPTH, D_MODEL, IN_COLS), D_MODEL ** -0.5),
        "conv_w": nrm(11, (DEPTH, CONV_W, D_CONV), CONV_W ** -0.5),
        "q_norm_w": gain(12, DK),
        "k_norm_w": gain(13, DK),
        "lambda_q1": nrm(14, (DEPTH, DK), 0.1),
        "lambda_k1": nrm(15, (DEPTH, DK), 0.1),
        "lambda_q2": nrm(16, (DEPTH, DK), 0.1),
        "lambda_k2": nrm(17, (DEPTH, DK), 0.1),
        "subln_w": gain(18, DV),
        "mem_norm_w": gain(19, D_MODEL),
        "w_mem_kv": nrm(20, (DEPTH, D_MODEL, 2 * CROSS_W), D_MODEL ** -0.5),
        "mq_norm_w": gain(21, MEM_HD),
        "mk_norm_w": gain(22, MEM_HD),
        "w_conv_out": nrm(23, (DEPTH, D_CONV, D_MODEL), D_CONV ** -0.5),
        "w_diff_out": nrm(24, (DEPTH, V_W, D_MODEL), V_W ** -0.5),
        "w_cross_out": nrm(25, (DEPTH, CROSS_W, D_MODEL), CROSS_W ** -0.5),
        "w_o": nrm(26, (DEPTH, D_MODEL, D_MODEL), D_MODEL ** -0.5),
        "norm2_w": gain(27, D_MODEL),
        "w_peer_q": nrm(28, (DEPTH, D_MODEL, PEER_HEADS * D_KEY), D_MODEL ** -0.5),
        "sub_keys": nrm(29, (DEPTH, 2, N_KEYS, D_KEY // 2), (D_KEY // 2) ** -0.5),
        "peer_u": nrm(30, (DEPTH, N_EXPERTS, D_MODEL), D_MODEL ** -0.5),
        "peer_v": nrm(31, (DEPTH, N_EXPERTS, D_MODEL), PEER_HEADS ** -0.5),
    }


def reference(x_prompt, x_sample, mem_prompt, cache_k, cache_v, page_table, state_conv,
              cache_mem_k, cache_mem_v, norm1_w, w_in, conv_w, q_norm_w, k_norm_w,
              lambda_q1, lambda_k1, lambda_q2, lambda_k2, subln_w, mem_norm_w, w_mem_kv,
              mq_norm_w, mk_norm_w, w_conv_out, w_diff_out, w_cross_out, w_o, norm2_w,
              w_peer_q, sub_keys, peer_u, peer_v):
    B, T = x_prompt.shape[:2]
    DB, TS = x_sample.shape[:2]
    pos_p = jnp.arange(T, dtype=jnp.int32)
    pos_s = PAST_LEN + jnp.arange(TS, dtype=jnp.int32)
    xp, xs = x_prompt, x_sample
    kp_l, vp_l, cp_l, mkp_l, mvp_l, ks_l, vs_l, cs_l = [], [], [], [], [], [], [], []
    for l in range(DEPTH):
        lam_init = 0.8 - 0.6 * math.exp(-0.3 * l)
        lam = (jnp.exp(jnp.sum(lambda_q1[l].astype(jnp.float32) * lambda_k1[l].astype(jnp.float32)))
               - jnp.exp(jnp.sum(lambda_q2[l].astype(jnp.float32) * lambda_k2[l].astype(jnp.float32)))
               + lam_init)
        layer_w = (norm1_w[l], w_in[l], conv_w[l], q_norm_w[l], k_norm_w[l], subln_w[l], mq_norm_w[l],
                   w_conv_out[l], w_diff_out[l], w_cross_out[l], w_o[l])
        mk_p, mv_p = memory_kv(mem_prompt, mem_norm_w[l], w_mem_kv[l], mk_norm_w[l])
        zero_buf = jnp.zeros((B, CONV_W - 1, D_CONV), dtype=xp.dtype)
        xp, kp, vp, cp = token_mixer(xp, pos_p, zero_buf, None, None, mk_p, mv_p, lam, lam_init, *layer_w)
        xp = xp + peer_ffn(rms_norm(xp, norm2_w[l]), w_peer_q[l], sub_keys[l], peer_u[l], peer_v[l])
        k_past = cache_k[l][page_table].reshape(DB, -1, N_HEADS, 2, DK)
        v_past = cache_v[l][page_table].reshape(DB, -1, N_HEADS, DV)
        xs, ks, vs, cs = token_mixer(xs, pos_s, state_conv[l], k_past, v_past, cache_mem_k[l], cache_mem_v[l],
                                     lam, lam_init, *layer_w)
        xs = xs + peer_ffn(rms_norm(xs, norm2_w[l]), w_peer_q[l], sub_keys[l], peer_u[l], peer_v[l])
        kp_l.append(kp); vp_l.append(vp); cp_l.append(cp); mkp_l.append(mk_p); mvp_l.append(mv_p)
        ks_l.append(ks); vs_l.append(vs); cs_l.append(cs)
    return (xp, xs, jnp.stack(kp_l), jnp.stack(vp_l), jnp.stack(cp_l), jnp.stack(mkp_l), jnp.stack(mvp_l),
            jnp.stack(ks_l), jnp.stack(vs_l), jnp.stack(cs_l))
```

```python
import functools
import math

import jax
import jax.numpy as jnp
import numpy as np
from jax import lax
from jax.experimental import pallas as pl
from jax.experimental.pallas import tpu as pltpu

F32 = jnp.float32
BF16 = jnp.bfloat16

D_MODEL = 2048
D_CONV = D_MODEL // 2
N_HEADS = 8
DK = D_MODEL // 32
DV = 2 * DK
HEAD_W = 2 * DK
QK_W = N_HEADS * 2 * DK
ROT_DIM = DK // 4
ROPE_THETA = 500000.0
N_MEM = 256
MEM_HEADS = 4
MEM_HD = D_MODEL // 8
CROSS_W = MEM_HEADS * MEM_HD
N_KEYS = 128
N_EXPERTS = N_KEYS * N_KEYS
PEER_HEADS = 8
PEER_TOPK = 16
D_KEY = 256
PAST_LEN = 16384
PAGE_SIZE = 128
EPS = 1e-6
NEG_INF = -1e30
LAM_INIT = 0.8 - 0.6 * math.exp(-0.3 * 0)
INV_SQRT2 = 1.0 / math.sqrt(2.0)

LANES = 128
SAMPLE_ROWS = 16
VMEM_LIMIT = 56 * 1024 * 1024


def _cparams(*sem):
    return pltpu.CompilerParams(dimension_semantics=sem, vmem_limit_bytes=VMEM_LIMIT)


def _rmsnorm_kernel(x_ref, w_ref, o_ref):
    x = x_ref[...]
    ms = jnp.mean(x * x, axis=-1, keepdims=True)
    o_ref[...] = (x * lax.rsqrt(ms + EPS) * w_ref[...]).astype(o_ref.dtype)


def rmsnorm_cast(x, w, tm):
    m, d = x.shape
    return pl.pallas_call(
        _rmsnorm_kernel,
        grid=(m // tm,),
        in_specs=[pl.BlockSpec((tm, d), lambda i: (i, 0)), pl.BlockSpec((1, d), lambda i: (0, 0))],
        out_specs=pl.BlockSpec((tm, d), lambda i: (i, 0)),
        out_shape=jax.ShapeDtypeStruct((m, d), BF16),
        compiler_params=_cparams("parallel"),
        name="rmsnorm_cast",
    )(x, w.reshape(1, d))


def _matmul_kernel(a_ref, b_ref, o_ref):
    o_ref[...] = jnp.dot(a_ref[...], b_ref[...], preferred_element_type=F32).astype(o_ref.dtype)


def matmul(a, b, out_dtype, tm, tn):
    m, k = a.shape
    _, n = b.shape
    tm = min(tm, m)
    tn = min(tn, n)
    return pl.pallas_call(
        _matmul_kernel,
        grid=(m // tm, n // tn),
        in_specs=[pl.BlockSpec((tm, k), lambda i, j: (i, 0)), pl.BlockSpec((k, tn), lambda i, j: (0, j))],
        out_specs=pl.BlockSpec((tm, tn), lambda i, j: (i, j)),
        out_shape=jax.ShapeDtypeStruct((m, n), out_dtype),
        compiler_params=_cparams("parallel", "arbitrary"),
        name="matmul",
    )(a, b)


def _group_norm_kernel(x_ref, w_ref, o_ref):
    for g in range(MEM_HEADS):
        c = x_ref[:, g * MEM_HD:(g + 1) * MEM_HD]
        ms = jnp.mean(c * c, axis=-1, keepdims=True)
        o_ref[:, g * MEM_HD:(g + 1) * MEM_HD] = c * lax.rsqrt(ms + EPS) * w_ref[...]


def group_norm(x, w, tm):
    m = x.shape[0]
    return pl.pallas_call(
        _group_norm_kernel,
        grid=(m // tm,),
        in_specs=[pl.BlockSpec((tm, CROSS_W), lambda i: (i, 0)), pl.BlockSpec((1, MEM_HD), lambda i: (0, 0))],
        out_specs=pl.BlockSpec((tm, CROSS_W), lambda i: (i, 0)),
        out_shape=jax.ShapeDtypeStruct((m, CROSS_W), F32),
        compiler_params=_cparams("parallel"),
        name="group_norm",
    )(x, w.reshape(1, MEM_HD))


def _sub_head_norm_rope(x, w, ones_bd, ca, cm, cp):
    x2 = x * x
    hi = x2.astype(BF16)
    lo = (x2 - hi.astype(F32)).astype(BF16)
    ss = jnp.dot(hi, ones_bd, preferred_element_type=F32) + jnp.dot(lo, ones_bd, preferred_element_type=F32)
    y = x * lax.rsqrt(ss * (1.0 / DK) + EPS) * w
    half = ROT_DIM // 2
    y_up = pltpu.roll(y, LANES - half, axis=1)
    y_dn = pltpu.roll(y, half, axis=1)
    return y * ca + y_up * cm + y_dn * cp


def _qk_rope_kernel(zq_ref, zk_ref, qw_ref, kw_ref, bd_ref, ca_ref, cm_ref, cp_ref, q_ref, k_ref, kb_ref):
    ones_bd = bd_ref[...]
    ca, cm, cp = ca_ref[...], cm_ref[...], cp_ref[...]
    for c in range(QK_W // LANES):
        sl = slice(c * LANES, (c + 1) * LANES)
        q = _sub_head_norm_rope(zq_ref[:, sl].astype(F32), qw_ref[...], ones_bd, ca, cm, cp)
        q_ref[:, sl] = (q * (DK ** -0.5)).astype(BF16)
        k = _sub_head_norm_rope(zk_ref[:, sl], kw_ref[...], ones_bd, ca, cm, cp)
        k_ref[:, sl] = k
        kb_ref[:, sl] = k.astype(BF16)


def _rope_tables(pos):
    half = ROT_DIM // 2
    inv_freq = ROPE_THETA ** (-jnp.arange(half, dtype=F32) / half)
    ang = pos.astype(F32)[:, None] * inv_freq[None, :]
    cos, sin = jnp.cos(ang), jnp.sin(ang)
    t = pos.shape[0]
    ones = jnp.ones((t, DK - ROT_DIM), F32)
    zeros = jnp.zeros((t, DK - ROT_DIM), F32)
    zh = jnp.zeros((t, half), F32)
    ca = jnp.concatenate([cos, cos, ones], axis=-1)
    cm = jnp.concatenate([-sin, zh, zeros], axis=-1)
    cp = jnp.concatenate([zh, sin, zeros], axis=-1)
    return tuple(jnp.concatenate([a, a], axis=-1) for a in (ca, cm, cp))


def qk_rope(z_a, z_b, q_norm_w, k_norm_w, tables, tm, table_blocks):
    m = z_a.shape[0]
    qw = jnp.tile(q_norm_w.reshape(1, DK), (1, 2))
    kw = jnp.tile(k_norm_w.reshape(1, DK), (1, 2))
    grp = np.arange(LANES) // DK
    ones_bd = jnp.asarray(grp[:, None] == grp[None, :], BF16)
    row_spec = lambda col: pl.BlockSpec((tm, QK_W), lambda i: (i, col))
    tab_spec = pl.BlockSpec((tm, LANES), lambda i: (i % table_blocks, 0))
    vec_spec = pl.BlockSpec((1, LANES), lambda i: (0, 0))
    return pl.pallas_call(
        _qk_rope_kernel,
        grid=(m // tm,),
        in_specs=[row_spec(3), row_spec(0), vec_spec, vec_spec,
                  pl.BlockSpec((LANES, LANES), lambda i: (0, 0)), tab_spec, tab_spec, tab_spec],
        out_specs=[row_spec(0), row_spec(0), row_spec(0)],
        out_shape=[jax.ShapeDtypeStruct((m, QK_W), BF16), jax.ShapeDtypeStruct((m, QK_W), F32),
                   jax.ShapeDtypeStruct((m, QK_W), BF16)],
        compiler_params=_cparams("parallel"),
        name="qk_rope",
    )(z_a, z_b, qw, kw, ones_bd, *tables)


def _lambda_value(lq1_ref, lk1_ref, lq2_ref, lk2_ref):
    a = jnp.sum(lq1_ref[...] * lk1_ref[...], axis=-1, keepdims=True)
    b = jnp.sum(lq2_ref[...] * lk2_ref[...], axis=-1, keepdims=True)
    return jnp.exp(a) - jnp.exp(b) + LAM_INIT


def _sub_layer_norm(o, sw):
    ms = jnp.mean(o * o, axis=-1, keepdims=True)
    return o * lax.rsqrt(ms + EPS) * sw * (1.0 - LAM_INIT)


def _diff_attn_kernel(lq1_ref, lk1_ref, lq2_ref, lk2_ref, sw_ref, q_ref, k_ref, v_ref, o_ref, *, seq, tq):
    lam = _lambda_value(lq1_ref, lk1_ref, lq2_ref, lk2_ref)
    sw = sw_ref[...]
    for i in range(seq // tq):
        n_k = (i + 1) * tq
        q = q_ref[i * tq:(i + 1) * tq, :]
        k = k_ref[0:n_k, :]
        v = v_ref[0:n_k, :].astype(BF16)
        row = lax.broadcasted_iota(jnp.int32, (tq, n_k), 0) + i * tq
        col = lax.broadcasted_iota(jnp.int32, (tq, n_k), 1)
        visible = col <= row

        def softmax_parts(qc, kc):
            s = lax.dot_general(qc, kc, (((1,), (1,)), ((), ())), preferred_element_type=F32)
            s = jnp.where(visible, s, NEG_INF)
            e = jnp.exp(s - jnp.max(s, axis=-1, keepdims=True))
            return e, jnp.sum(e, axis=-1, keepdims=True)

        e1, l1 = softmax_parts(q[:, :DK], k[:, :DK])
        e2, l2 = softmax_parts(q[:, DK:], k[:, DK:])
        a = e1 * (1.0 / l1) - e2 * (lam / l2)
        o = jnp.dot(a.astype(BF16), v, preferred_element_type=F32)
        o_ref[i * tq:(i + 1) * tq, :] = _sub_layer_norm(o, sw).astype(o_ref.dtype)


def diff_attn_prompt(q, k, z_b, lam_vecs, subln_w, batch, seq, tq):
    vec = pl.BlockSpec((1, DK), lambda b, h: (0, 0))
    head = lambda off: pl.BlockSpec((seq, HEAD_W), lambda b, h: (b, h + off))
    return pl.pallas_call(
        functools.partial(_diff_attn_kernel, seq=seq, tq=tq),
        grid=(batch, N_HEADS),
        in_specs=[vec, vec, vec, vec, pl.BlockSpec((1, DV), lambda b, h: (0, 0)),
                  head(0), head(0), head(N_HEADS)],
        out_specs=head(0),
        out_shape=jax.ShapeDtypeStruct((batch * seq, QK_W), BF16),
        compiler_params=_cparams("parallel", "parallel"),
        name="diff_attn_prompt",
    )(*lam_vecs, subln_w.reshape(1, DV), q, k, z_b)


def _diff_attn_decode_kernel(pt_ref, lq1_ref, lk1_ref, lq2_ref, lk2_ref, sw_ref, q_ref, kn_ref, vn_ref, *rest,
                             pages_per_step):
    del pt_ref
    k_refs = rest[:pages_per_step]
    v_refs = rest[pages_per_step:2 * pages_per_step]
    o_ref, qs_ref, m_ref, l_ref, acc_ref = rest[2 * pages_per_step:]
    j = pl.program_id(1)
    n_sub = 2 * N_HEADS

    @pl.when(j == 0)
    def _():
        sub = lax.broadcasted_iota(jnp.int32, (n_sub, QK_W), 1) // DK
        own = sub == lax.broadcasted_iota(jnp.int32, (n_sub, QK_W), 0)
        qrows = jnp.where(own, q_ref[0:1, :].astype(F32), 0.0)
        qs_ref[...] = qrows.astype(BF16)
        k_new = kn_ref[0:1, :].astype(BF16).astype(F32)
        s_new = jnp.sum(qrows * k_new, axis=-1, keepdims=True)
        m_ref[...] = jnp.broadcast_to(s_new, m_ref.shape)
        l_ref[...] = jnp.ones(l_ref.shape, F32)
        acc_ref[...] = jnp.broadcast_to(vn_ref[0:1, :], acc_ref.shape)

    qs = qs_ref[...]
    s = jnp.concatenate(
        [lax.dot_general(qs, k_refs[r][...].astype(BF16), (((1,), (1,)), ((), ())), preferred_element_type=F32)
         for r in range(pages_per_step)], axis=-1)
    m_old = m_ref[:, 0:1]
    m_new = jnp.maximum(m_old, jnp.max(s, axis=-1, keepdims=True))
    alpha = jnp.exp(m_old - m_new)
    p = jnp.exp(s - m_new)
    l_ref[...] = jnp.broadcast_to(alpha * l_ref[:, 0:1] + jnp.sum(p, axis=-1, keepdims=True), l_ref.shape)
    m_ref[...] = jnp.broadcast_to(m_new, m_ref.shape)
    pv = acc_ref[...] * alpha
    for r in range(pages_per_step):
        pv = pv + jnp.dot(p[:, r * PAGE_SIZE:(r + 1) * PAGE_SIZE].astype(BF16), v_refs[r][...].astype(BF16),
                          preferred_element_type=F32)
    acc_ref[...] = pv

    @pl.when(j == pl.num_programs(1) - 1)
    def _():
        lam = _lambda_value(lq1_ref, lk1_ref, lq2_ref, lk2_ref)
        o = acc_ref[...] / l_ref[:, 0:1]
        for h in range(N_HEADS):
            sl = slice(h * DV, (h + 1) * DV)
            od = o[2 * h:2 * h + 1, sl] - lam * o[2 * h + 1:2 * h + 2, sl]
            o_ref[:, sl] = jnp.broadcast_to(_sub_layer_norm(od, sw_ref[...]), (SAMPLE_ROWS, DV)).astype(o_ref.dtype)


def diff_attn_decode(q, k_new, z_b, cache_k, cache_v, page_table, lam_vecs, subln_w, pages_per_step):
    n_batch, n_pages = page_table.shape
    vec = pl.BlockSpec((1, DK), lambda b, j, pt: (0, 0))
    tok = lambda col: pl.BlockSpec((SAMPLE_ROWS, QK_W), lambda b, j, pt: (b, col))

    def page_spec(r):
        return pl.BlockSpec((None, PAGE_SIZE, QK_W), lambda b, j, pt: (pt[b, j * pages_per_step + r], 0, 0))

    grid_spec = pltpu.PrefetchScalarGridSpec(
        num_scalar_prefetch=1,
        grid=(n_batch, n_pages // pages_per_step),
        in_specs=[vec, vec, vec, vec, pl.BlockSpec((1, DV), lambda b, j, pt: (0, 0)), tok(0), tok(0), tok(1)]
        + [page_spec(r) for r in range(pages_per_step)] + [page_spec(r) for r in range(pages_per_step)],
        out_specs=tok(0),
        scratch_shapes=[pltpu.VMEM((2 * N_HEADS, QK_W), BF16), pltpu.VMEM((2 * N_HEADS, LANES), F32),
                        pltpu.VMEM((2 * N_HEADS, LANES), F32), pltpu.VMEM((2 * N_HEADS, QK_W), F32)],
    )
    return pl.pallas_call(
        functools.partial(_diff_attn_decode_kernel, pages_per_step=pages_per_step),
        grid_spec=grid_spec,
        out_shape=jax.ShapeDtypeStruct((n_batch * SAMPLE_ROWS, QK_W), BF16),
        compiler_params=_cparams("parallel", "arbitrary"),
        name="diff_attn_decode",
    )(page_table, *lam_vecs, subln_w.reshape(1, DV), q, k_new, z_b,
      *([cache_k] * pages_per_step), *([cache_v] * pages_per_step))


def _mem_attn_kernel(q_ref, w_ref, mk_ref, mv_ref, o_ref):
    q = q_ref[...].astype(F32)
    ms = jnp.mean(q * q, axis=-1, keepdims=True)
    qn = (q * lax.rsqrt(ms + EPS) * w_ref[...] * (MEM_HD ** -0.5)).astype(BF16)
    s = lax.dot_general(qn, mk_ref[...].astype(BF16), (((1,), (1,)), ((), ())), preferred_element_type=F32)
    e = jnp.exp(s - jnp.max(s, axis=-1, keepdims=True))
    p = e * (1.0 / jnp.sum(e, axis=-1, keepdims=True))
    o_ref[...] = jnp.dot(p.astype(BF16), mv_ref[...].astype(BF16), preferred_element_type=F32).astype(o_ref.dtype)


def mem_attn(z_q, q_col, mq_norm_w, mk, mv, batch, seq, tq):
    nq = seq // tq
    return pl.pallas_call(
        _mem_attn_kernel,
        grid=(batch, MEM_HEADS, nq),
        in_specs=[pl.BlockSpec((tq, MEM_HD), lambda b, h, i: (b * nq + i, q_col + h)),
                  pl.BlockSpec((1, MEM_HD), lambda b, h, i: (0, 0)),
                  pl.BlockSpec((N_MEM, MEM_HD), lambda b, h, i: (b, h)),
                  pl.BlockSpec((N_MEM, MEM_HD), lambda b, h, i: (b, h))],
        out_specs=pl.BlockSpec((tq, MEM_HD), lambda b, h, i: (b * nq + i, h)),
        out_shape=jax.ShapeDtypeStruct((batch * seq, CROSS_W), BF16),
        compiler_params=_cparams("parallel", "parallel", "parallel"),
        name="mem_attn",
    )(z_q, mq_norm_w.reshape(1, MEM_HD), mk, mv)


def _merge_tail(zb, conv, od_ref, om_ref, g0_ref, g1_ref, g2_ref, wc_ref, wd_ref, wx_ref, o_ref):
    yc = jnp.dot((zb * conv).astype(BF16), wc_ref[...], preferred_element_type=F32)
    yd = jnp.dot(od_ref[...], wd_ref[...], preferred_element_type=F32)
    ym = jnp.dot(om_ref[...], wx_ref[...], preferred_element_type=F32)
    merged = (jax.nn.sigmoid(g0_ref[...].astype(F32)) * yc + jax.nn.sigmoid(g1_ref[...].astype(F32)) * yd
              + jax.nn.sigmoid(g2_ref[...].astype(F32)) * ym)
    o_ref[...] = merged.astype(o_ref.dtype)


def _merge_prompt_kernel(zb_ref, zc_ref, zx_ref, hc_ref, hx_ref, cw_ref, *rest, tiles_per_seq, tm):
    u = zc_ref[...].astype(F32) * zx_ref[...].astype(F32)
    first = pl.program_id(0) % tiles_per_seq == 0
    halo = jnp.where(first, 0.0, hc_ref[...].astype(F32) * hx_ref[...].astype(F32))
    h1 = halo[SAMPLE_ROWS - 1:SAMPLE_ROWS, :]
    h2 = halo[SAMPLE_ROWS - 2:SAMPLE_ROWS - 1, :]
    row = lax.broadcasted_iota(jnp.int32, u.shape, 0)
    u1 = jnp.where(row == 0, h1, pltpu.roll(u, 1, axis=0))
    u2 = jnp.where(row == 0, h2, jnp.where(row == 1, h1, pltpu.roll(u, 2, axis=0)))
    conv = cw_ref[0:1, :] * u2 + cw_ref[1:2, :] * u1 + cw_ref[2:3, :] * u
    _merge_tail(zb_ref[...].astype(F32), conv, *rest)


def _merge_sample_kernel(zb_ref, zc_ref, zx_ref, p2_ref, p1_ref, cw_ref, *rest):
    u = zc_ref[...].astype(F32) * zx_ref[...].astype(F32)
    conv = cw_ref[0:1, :] * p2_ref[...] + cw_ref[1:2, :] * p1_ref[...] + cw_ref[2:3, :] * u
    _merge_tail(zb_ref[...].astype(F32), conv, *rest)


def merge_branches(z_a, prev, conv_w, o_diff, o_mem, z_g, w_conv_out, w_diff_out, w_cross_out, tm, seq):
    m = z_a.shape[0]
    col = lambda c, w: pl.BlockSpec((tm, w), lambda i: (i, c))
    const = lambda shape: pl.BlockSpec(shape, lambda i: (0, 0))
    tail_specs = [col(0, D_CONV), col(0, CROSS_W), col(0, D_MODEL), col(1, D_MODEL), col(2, D_MODEL),
                  const((D_CONV, D_MODEL)), const((QK_W, D_MODEL)), const((CROSS_W, D_MODEL))]
    tail_args = (o_diff, o_mem, z_g, z_g, z_g, w_conv_out, w_diff_out, w_cross_out)
    if prev is None:
        blocks = tm // SAMPLE_ROWS
        halo = lambda c: pl.BlockSpec((SAMPLE_ROWS, D_CONV), lambda i: (jnp.maximum(i * blocks - 1, 0), c))
        kern = functools.partial(_merge_prompt_kernel, tiles_per_seq=seq // tm, tm=tm)
        head_specs = [col(0, D_CONV), col(1, D_CONV), col(2, D_CONV), halo(1), halo(2), const((3, D_CONV))]
        head_args = (z_a, z_a, z_a, z_a, z_a, conv_w)
    else:
        kern = _merge_sample_kernel
        head_specs = [col(0, D_CONV), col(1, D_CONV), col(2, D_CONV), col(0, D_CONV), col(0, D_CONV),
                      const((3, D_CONV))]
        head_args = (z_a, z_a, z_a, prev[0], prev[1], conv_w)
    return pl.pallas_call(
        kern,
        grid=(m // tm,),
        in_specs=head_specs + tail_specs,
        out_specs=col(0, D_MODEL),
        out_shape=jax.ShapeDtypeStruct((m, D_MODEL), BF16),
        compiler_params=_cparams("parallel"),
        name="merge_branches",
    )(*head_args, *tail_args)


def _conv_state_kernel(zc_ref, zx_ref, o_ref):
    u = zc_ref[...].astype(F32) * zx_ref[...].astype(F32)
    o_ref[...] = u[SAMPLE_ROWS - 2:SAMPLE_ROWS, :]


def conv_state_prompt(z_a, batch, seq):
    blocks = seq // SAMPLE_ROWS
    tail = lambda c: pl.BlockSpec((SAMPLE_ROWS, D_CONV), lambda b: (b * blocks + blocks - 1, c))
    return pl.pallas_call(
        _conv_state_kernel,
        grid=(batch,),
        in_specs=[tail(1), tail(2)],
        out_specs=pl.BlockSpec((None, 2, D_CONV), lambda b: (b, 0, 0)),
        out_shape=jax.ShapeDtypeStruct((batch, 2, D_CONV), F32),
        compiler_params=_cparams("parallel"),
        name="conv_state_prompt",
    )(z_a, z_a)


def _sample_u_kernel(zc_ref, zx_ref, o_ref):
    o_ref[...] = zc_ref[...].astype(F32) * zx_ref[...].astype(F32)


def sample_u(z_a):
    m = z_a.shape[0]
    blk = lambda c: pl.BlockSpec((m, D_CONV), lambda i: (0, c))
    return pl.pallas_call(
        _sample_u_kernel,
        grid=(1,),
        in_specs=[blk(1), blk(2)],
        out_specs=blk(0),
        out_shape=jax.ShapeDtypeStruct((m, D_CONV), F32),
        compiler_params=_cparams("arbitrary"),
        name="sample_u",
    )(z_a, z_a)


def _out_proj_kernel(x_ref, a_ref, w_ref, nw_ref, x1_ref, h_ref):
    x1 = x_ref[...] + jnp.dot(a_ref[...], w_ref[...], preferred_element_type=F32)
    x1_ref[...] = x1
    ms = jnp.mean(x1 * x1, axis=-1, keepdims=True)
    h_ref[...] = (x1 * lax.rsqrt(ms + EPS) * nw_ref[...]).astype(h_ref.dtype)


def out_proj(x, merged, w_o, norm2_w, tm):
    m, d = x.shape
    tm = min(tm, m)
    row = pl.BlockSpec((tm, d), lambda i: (i, 0))
    return pl.pallas_call(
        _out_proj_kernel,
        grid=(m // tm,),
        in_specs=[row, row, pl.BlockSpec((d, d), lambda i: (0, 0)), pl.BlockSpec((1, d), lambda i: (0, 0))],
        out_specs=[row, row],
        out_shape=[jax.ShapeDtypeStruct((m, d), F32), jax.ShapeDtypeStruct((m, d), BF16)],
        compiler_params=_cparams("parallel"),
        name="out_proj",
    )(x, merged, w_o, norm2_w.reshape(1, d))


def _top_values(x, count):
    vals = []
    for _ in range(count):
        m = jnp.max(x, axis=0, keepdims=True)
        vals.append(m)
        x = jnp.where(x == m, NEG_INF, x)
    return vals


def _peer_tables(qp_ref, sk_ref, s1_ref, e1_ref, s2_ref, e2_ref, t_ref):
    for h in range(PEER_HEADS):
        s = []
        for c in range(2):
            qc = qp_ref[:, (2 * h + c) * N_KEYS:(2 * h + c + 1) * N_KEYS]
            s.append(lax.dot_general(sk_ref[c], qc, (((1,), (1,)), ((), ())), preferred_element_type=F32))
        v1 = _top_values(s[0], PEER_TOPK)
        v2 = _top_values(s[1], PEER_TOPK)
        cands = []
        for a in range(PEER_TOPK):
            for b in range(PEER_TOPK // (a + 1)):
                cands.append(v1[a] + v2[b])
        cand = jnp.concatenate(cands, axis=0)
        top = _top_values(cand, PEER_TOPK)
        smax = v1[0] + v2[0]
        z = jnp.zeros_like(smax)
        for r in range(PEER_TOPK):
            z = z + jnp.exp(top[r] - smax)
        rows = slice(h * N_KEYS, (h + 1) * N_KEYS)
        s1_ref[rows, :] = s[0]
        s2_ref[rows, :] = s[1]
        e1_ref[rows, :] = jnp.exp(s[0] - v1[0]) / z
        e2_ref[rows, :] = jnp.exp(s[1] - v2[0])
        t_ref[h:h + 1, :] = top[PEER_TOPK - 1]


def _peer_kernel(h_ref, qp_ref, sk_ref, x1_ref, u_ref, vt_ref, o_ref,
                 acc_ref, s1_ref, e1_ref, s2_ref, e2_ref, t_ref, *, ec):
    c = pl.program_id(1)

    @pl.when(c == 0)
    def _():
        acc_ref[...] = jnp.zeros(acc_ref.shape, F32)
        _peer_tables(qp_ref, sk_ref, s1_ref, e1_ref, s2_ref, e2_ref, t_ref)

    a = lax.dot_general(u_ref[...], h_ref[...], (((1,), (1,)), ((), ())), preferred_element_type=F32)
    act = 0.5 * a * (1.0 + lax.erf(a * INV_SQRT2))
    chunks = ec // N_KEYS
    ws = []
    for il in range(chunks):
        gate = None
        for h in range(PEER_HEADS):
            row = h * N_KEYS + c * chunks + il
            s1 = s1_ref[pl.ds(row, 1), :]
            e1 = e1_ref[pl.ds(row, 1), :]
            keys = slice(h * N_KEYS, (h + 1) * N_KEYS)
            pair = s1 + s2_ref[keys, :]
            g = jnp.where(pair >= t_ref[h:h + 1, :], e1 * e2_ref[keys, :], 0.0)
            gate = g if gate is None else gate + g
        ws.append((act[il * N_KEYS:(il + 1) * N_KEYS, :] * gate).astype(BF16))
    w = jnp.concatenate(ws, axis=0)
    acc_ref[...] += jnp.dot(vt_ref[...], w, preferred_element_type=F32)

    @pl.when(c == pl.num_programs(1) - 1)
    def _():
        o_ref[...] = x1_ref[...] + acc_ref[...].T


def peer_layer(x1, h2, qp, sub_keys, peer_u, peer_vt, tb, ec):
    m, d = x1.shape
    tb = min(tb, m)
    tok = lambda: pl.BlockSpec((tb, d), lambda n, c: (n, 0))
    tab = lambda: pltpu.VMEM((PEER_HEADS * N_KEYS, tb), F32)
    return pl.pallas_call(
        functools.partial(_peer_kernel, ec=ec),
        grid=(m // tb, N_EXPERTS // ec),
        in_specs=[tok(), tok(), pl.BlockSpec((2, N_KEYS, D_KEY // 2), lambda n, c: (0, 0, 0)), tok(),
                  pl.BlockSpec((ec, d), lambda n, c: (c, 0)), pl.BlockSpec((d, ec), lambda n, c: (0, c))],
        out_specs=tok(),
        out_shape=jax.ShapeDtypeStruct((m, d), F32),
        scratch_shapes=[pltpu.VMEM((d, tb), F32), tab(), tab(), tab(), tab(), pltpu.VMEM((PEER_HEADS, tb), F32)],
        compiler_params=_cparams("parallel", "arbitrary"),
        name="peer_layer",
    )(h2, qp, sub_keys, x1, peer_u, peer_vt)


def _token_stack(x, rope_tabs, table_blocks, w, tiles):
    h = rmsnorm_cast(x, w["norm1_w"], tiles["norm"])
    z_a = matmul(h, w["w_in_a"], BF16, tiles["mm"], 1024)
    z_b = matmul(h, w["w_in_b"], F32, tiles["mm"], 1024)
    z_g = matmul(h, w["w_in_g"], BF16, tiles["mm"], 1024)
    q, k, k_bf = qk_rope(z_a, z_b, w["q_norm_w"], w["k_norm_w"], rope_tabs, tiles["rope"], table_blocks)
    return z_a, z_b, z_g, q, k, k_bf


def _finish(x, merged, w, tiles):
    x1, h2 = out_proj(x, merged, w["w_o"], w["norm2_w"], tiles["out"])
    qp = matmul(h2, w["w_peer_q"], BF16, tiles["mm"], 1024)
    return peer_layer(x1, h2, qp, w["sub_keys"], w["peer_u"], w["peer_vt"], tiles["peer_tb"], tiles["peer_ec"])


def kernel(x_prompt, x_sample, mem_prompt, cache_k, cache_v, page_table, state_conv, cache_mem_k, cache_mem_v,
           norm1_w, w_in, conv_w, q_norm_w, k_norm_w, lambda_q1, lambda_k1, lambda_q2, lambda_k2, subln_w,
           mem_norm_w, w_mem_kv, mq_norm_w, mk_norm_w, w_conv_out, w_diff_out, w_cross_out, w_o, norm2_w,
           w_peer_q, sub_keys, peer_u, peer_v):
    batch, seq, d = x_prompt.shape
    n_dec = x_sample.shape[0]
    l = 0
    gates_at = 3 * D_CONV + 2 * QK_W + QK_W + CROSS_W
    memq_at = gates_at - CROSS_W
    w = {
        "norm1_w": norm1_w[l], "norm2_w": norm2_w[l], "q_norm_w": q_norm_w[l], "k_norm_w": k_norm_w[l],
        "w_in_a": w_in[l, :, :3 * D_CONV + QK_W].astype(BF16),
        "w_in_b": w_in[l, :, 3 * D_CONV + QK_W:memq_at].astype(BF16),
        "w_in_g": jnp.concatenate([w_in[l, :, gates_at:], w_in[l, :, memq_at:gates_at]], axis=-1).astype(BF16),
        "w_o": w_o[l].astype(BF16), "w_peer_q": w_peer_q[l].astype(BF16), "sub_keys": sub_keys[l].astype(BF16),
        "peer_u": peer_u[l].astype(BF16), "peer_vt": peer_v[l].T.astype(BF16),
    }
    w_conv_out_b = w_conv_out[l].astype(BF16)
    w_diff_out_b = w_diff_out[l].astype(BF16)
    w_cross_out_b = w_cross_out[l].astype(BF16)
    lam_vecs = tuple(v[l].reshape(1, DK) for v in (lambda_q1, lambda_k1, lambda_q2, lambda_k2))
    memq_col = (3 * D_MODEL) // MEM_HD

    tiles_p = {"norm": 512, "mm": 1024, "rope": 512, "out": 512, "peer_tb": 512, "peer_ec": 512}
    xp = x_prompt.reshape(batch * seq, d)
    tabs_p = _rope_tables(jnp.arange(seq, dtype=jnp.int32))
    z_a, z_b, z_g, q, k, k_bf = _token_stack(xp, tabs_p, seq // tiles_p["rope"], w, tiles_p)
    h_mem = rmsnorm_cast(mem_prompt.reshape(batch * N_MEM, d), mem_norm_w[l], 512)
    kv_mem = matmul(h_mem, w_mem_kv[l].astype(BF16), F32, 1024, 1024)
    mk_p = group_norm(kv_mem, mk_norm_w[l], 512)
    mv_p = kv_mem[:, CROSS_W:]
    o_diff = diff_attn_prompt(q, k_bf, z_b, lam_vecs, subln_w[l], batch, seq, 256)
    o_mem = mem_attn(z_g, memq_col, mq_norm_w[l], mk_p, mv_p, batch, seq, 1024)
    merged = merge_branches(z_a, None, conv_w[l], o_diff, o_mem, z_g, w_conv_out_b, w_diff_out_b, w_cross_out_b,
                            256, seq)
    conv_p = conv_state_prompt(z_a, batch, seq)
    y_prompt = _finish(xp, merged, w, tiles_p).reshape(batch, seq, d)

    tiles_s = {"norm": 128, "mm": 128, "rope": 128, "out": 128, "peer_tb": 128, "peer_ec": 512}
    m_s = n_dec * SAMPLE_ROWS
    xs = jnp.pad(x_sample, ((0, 0), (0, SAMPLE_ROWS - x_sample.shape[1]), (0, 0))).reshape(m_s, d)
    tabs_s = _rope_tables(jnp.full((m_s,), PAST_LEN, jnp.int32))
    zs_a, zs_b, zs_g, qs, ks, _ = _token_stack(xs, tabs_s, 1, w, tiles_s)
    pages = cache_k.shape[1]
    os_diff = diff_attn_decode(qs, ks, zs_b, cache_k[l].reshape(pages, PAGE_SIZE, QK_W),
                               cache_v[l].reshape(pages, PAGE_SIZE, QK_W), page_table, lam_vecs, subln_w[l], 8)
    os_mem = mem_attn(zs_g, memq_col, mq_norm_w[l], cache_mem_k[l].reshape(n_dec * N_MEM, CROSS_W),
                      cache_mem_v[l].reshape(n_dec * N_MEM, CROSS_W), n_dec, SAMPLE_ROWS, SAMPLE_ROWS)
    pad_state = lambda r: jnp.pad(state_conv[l][:, r:r + 1], ((0, 0), (0, SAMPLE_ROWS - 1), (0, 0))).reshape(m_s, D_CONV)
    merged_s = merge_branches(zs_a, (pad_state(0), pad_state(1)), conv_w[l], os_diff, os_mem, zs_g,
                              w_conv_out_b, w_diff_out_b, w_cross_out_b, 128, SAMPLE_ROWS)
    us = sample_u(zs_a).reshape(n_dec, SAMPLE_ROWS, D_CONV)[:, 0:1]
    conv_s = jnp.concatenate([state_conv[l][:, 1:2], us], axis=1)
    y_sample = _finish(xs, merged_s, w, tiles_s).reshape(n_dec, SAMPLE_ROWS, d)[:, 0:1]

    first = lambda a, n: a.reshape(n, SAMPLE_ROWS, N_HEADS, HEAD_W)[:, 0:1]
    return (y_prompt, y_sample,
            k.reshape(1, batch, seq, N_HEADS, HEAD_W), z_b[:, QK_W:].reshape(1, batch, seq, N_HEADS, DV),
            conv_p[None], mk_p.reshape(1, batch, N_MEM, MEM_HEADS, MEM_HD),
            mv_p.reshape(1, batch, N_MEM, MEM_HEADS, MEM_HD),
            first(ks, n_dec)[None], first(zs_b[:, QK_W:], n_dec)[None], conv_s[None])
```

```python
import functools
import math

import jax
import jax.numpy as jnp
import numpy as np
from jax import lax
from jax.experimental import pallas as pl
from jax.experimental.pallas import tpu as pltpu

F32 = jnp.float32
BF16 = jnp.bfloat16

D_MODEL = 2048
D_CONV = D_MODEL // 2
N_HEADS = 8
DK = D_MODEL // 32
DV = 2 * DK
HEAD_W = 2 * DK
QK_W = N_HEADS * 2 * DK
ROT_DIM = DK // 4
ROPE_THETA = 500000.0
N_MEM = 256
MEM_HEADS = 4
MEM_HD = D_MODEL // 8
CROSS_W = MEM_HEADS * MEM_HD
N_KEYS = 128
N_EXPERTS = N_KEYS * N_KEYS
PEER_HEADS = 8
PEER_TOPK = 16
D_KEY = 256
PAST_LEN = 16384
PAGE_SIZE = 128
EPS = 1e-6
NEG_INF = -1e30
KV_AT = 3 * D_CONV + QK_W
MEMQ_AT = KV_AT + 2 * QK_W
GATES_AT = MEMQ_AT + CROSS_W
LAM_INIT = 0.8 - 0.6 * math.exp(-0.3 * 0)
INV_SQRT2 = 1.0 / math.sqrt(2.0)

LANES = 128
SAMPLE_ROWS = 16
VMEM_LIMIT = 56 * 1024 * 1024


def _cparams(*sem, flags=None):
    return pltpu.CompilerParams(dimension_semantics=sem, vmem_limit_bytes=VMEM_LIMIT, flags=flags)


def _rmsnorm_kernel(x_ref, w_ref, o_ref):
    x = x_ref[...]
    ms = jnp.mean(x * x, axis=-1, keepdims=True)
    o_ref[...] = (x * lax.rsqrt(ms + EPS) * w_ref[...]).astype(o_ref.dtype)


def rmsnorm_cast(x, w, tm):
    m, d = x.shape
    return pl.pallas_call(
        _rmsnorm_kernel,
        grid=(m // tm,),
        in_specs=[pl.BlockSpec((tm, d), lambda i: (i, 0)), pl.BlockSpec((1, d), lambda i: (0, 0))],
        out_specs=pl.BlockSpec((tm, d), lambda i: (i, 0)),
        out_shape=jax.ShapeDtypeStruct((m, d), BF16),
        compiler_params=_cparams("parallel"),
        name="rmsnorm_cast",
    )(x, w.reshape(1, d))


def _matmul_kernel(a_ref, b_ref, o_ref):
    o_ref[...] = jnp.dot(a_ref[...], b_ref[...], preferred_element_type=F32).astype(o_ref.dtype)


def matmul(a, b, out_dtype, tm, tn, col0=0, n=None):
    m, k = a.shape
    n = b.shape[1] if n is None else n
    tm = min(tm, m)
    tn = min(tn, n)
    first = col0 // tn
    return pl.pallas_call(
        _matmul_kernel,
        grid=(m // tm, n // tn),
        in_specs=[pl.BlockSpec((tm, k), lambda i, j: (i, 0)), pl.BlockSpec((k, tn), lambda i, j: (0, j + first))],
        out_specs=pl.BlockSpec((tm, tn), lambda i, j: (i, j)),
        out_shape=jax.ShapeDtypeStruct((m, n), out_dtype),
        compiler_params=_cparams("parallel", "arbitrary"),
        name="matmul",
    )(a, b)


def _group_norm_kernel(x_ref, w_ref, o_ref):
    for g in range(MEM_HEADS):
        c = x_ref[:, g * MEM_HD:(g + 1) * MEM_HD]
        ms = jnp.mean(c * c, axis=-1, keepdims=True)
        o_ref[:, g * MEM_HD:(g + 1) * MEM_HD] = c * lax.rsqrt(ms + EPS) * w_ref[...]


def group_norm(x, w, tm):
    m = x.shape[0]
    return pl.pallas_call(
        _group_norm_kernel,
        grid=(m // tm,),
        in_specs=[pl.BlockSpec((tm, CROSS_W), lambda i: (i, 0)), pl.BlockSpec((1, MEM_HD), lambda i: (0, 0))],
        out_specs=pl.BlockSpec((tm, CROSS_W), lambda i: (i, 0)),
        out_shape=jax.ShapeDtypeStruct((m, CROSS_W), F32),
        compiler_params=_cparams("parallel"),
        name="group_norm",
    )(x, w.reshape(1, MEM_HD))


def _sub_head_norm_rope(x, w, ones_bd, ca, cm, cp):
    x2 = x * x
    hi = x2.astype(BF16)
    lo = (x2 - hi.astype(F32)).astype(BF16)
    ss = jnp.dot(hi, ones_bd, preferred_element_type=F32) + jnp.dot(lo, ones_bd, preferred_element_type=F32)
    y = x * lax.rsqrt(ss * (1.0 / DK) + EPS) * w
    half = ROT_DIM // 2
    y_up = pltpu.roll(y, LANES - half, axis=1)
    y_dn = pltpu.roll(y, half, axis=1)
    return y * ca + y_up * cm + y_dn * cp


def _qk_rope_kernel(zq_ref, zk_ref, qw_ref, kw_ref, bd_ref, ca_ref, cm_ref, cp_ref, q_ref, k_ref, kb_ref):
    ones_bd = bd_ref[...]
    ca, cm, cp = ca_ref[...], cm_ref[...], cp_ref[...]
    for c in range(QK_W // LANES):
        sl = slice(c * LANES, (c + 1) * LANES)
        q = _sub_head_norm_rope(zq_ref[:, sl].astype(F32), qw_ref[...], ones_bd, ca, cm, cp)
        q_ref[:, sl] = (q * (DK ** -0.5)).astype(BF16)
        k = _sub_head_norm_rope(zk_ref[:, sl], kw_ref[...], ones_bd, ca, cm, cp)
        k_ref[:, sl] = k
        kb_ref[:, sl] = k.astype(BF16)


def _rope_tables(pos):
    half = ROT_DIM // 2
    inv_freq = ROPE_THETA ** (-jnp.arange(half, dtype=F32) / half)
    ang = pos.astype(F32)[:, None] * inv_freq[None, :]
    cos, sin = jnp.cos(ang), jnp.sin(ang)
    t = pos.shape[0]
    ones = jnp.ones((t, DK - ROT_DIM), F32)
    zeros = jnp.zeros((t, DK - ROT_DIM), F32)
    zh = jnp.zeros((t, half), F32)
    ca = jnp.concatenate([cos, cos, ones], axis=-1)
    cm = jnp.concatenate([-sin, zh, zeros], axis=-1)
    cp = jnp.concatenate([zh, sin, zeros], axis=-1)
    return tuple(jnp.concatenate([a, a], axis=-1) for a in (ca, cm, cp))


def qk_rope(z_a, z_b, q_norm_w, k_norm_w, tables, tm, table_blocks):
    m = z_a.shape[0]
    qw = jnp.tile(q_norm_w.reshape(1, DK), (1, 2))
    kw = jnp.tile(k_norm_w.reshape(1, DK), (1, 2))
    grp = np.arange(LANES) // DK
    ones_bd = jnp.asarray(grp[:, None] == grp[None, :], BF16)
    row_spec = lambda col: pl.BlockSpec((tm, QK_W), lambda i: (i, col))
    tab_spec = pl.BlockSpec((tm, LANES), lambda i: (i % table_blocks, 0))
    vec_spec = pl.BlockSpec((1, LANES), lambda i: (0, 0))
    return pl.pallas_call(
        _qk_rope_kernel,
        grid=(m // tm,),
        in_specs=[row_spec(3), row_spec(0), vec_spec, vec_spec,
                  pl.BlockSpec((LANES, LANES), lambda i: (0, 0)), tab_spec, tab_spec, tab_spec],
        out_specs=[row_spec(0), row_spec(0), row_spec(0)],
        out_shape=[jax.ShapeDtypeStruct((m, QK_W), BF16), jax.ShapeDtypeStruct((m, QK_W), F32),
                   jax.ShapeDtypeStruct((m, QK_W), BF16)],
        compiler_params=_cparams("parallel"),
        name="qk_rope",
    )(z_a, z_b, qw, kw, ones_bd, *tables)


def _lambda_value(lq1_ref, lk1_ref, lq2_ref, lk2_ref):
    a = jnp.sum(lq1_ref[...] * lk1_ref[...], axis=-1, keepdims=True)
    b = jnp.sum(lq2_ref[...] * lk2_ref[...], axis=-1, keepdims=True)
    return jnp.exp(a) - jnp.exp(b) + LAM_INIT


def _sub_layer_norm(o, sw):
    ms = jnp.mean(o * o, axis=-1, keepdims=True)
    return o * lax.rsqrt(ms + EPS) * sw * (1.0 - LAM_INIT)


def _diff_attn_kernel(lq1_ref, lk1_ref, lq2_ref, lk2_ref, sw_ref, q_ref, k_ref, v_ref, o_ref, *, seq, tq):
    lam = _lambda_value(lq1_ref, lk1_ref, lq2_ref, lk2_ref)
    sw = sw_ref[...]
    for i in range(seq // tq):
        n_k = (i + 1) * tq
        q = q_ref[i * tq:(i + 1) * tq, :]
        k = k_ref[0:n_k, :]
        v = v_ref[0:n_k, :].astype(BF16)
        row = lax.broadcasted_iota(jnp.int32, (tq, n_k), 0) + i * tq
        col = lax.broadcasted_iota(jnp.int32, (tq, n_k), 1)
        visible = col <= row

        def softmax_parts(qc, kc):
            s = lax.dot_general(qc, kc, (((1,), (1,)), ((), ())), preferred_element_type=F32)
            s = jnp.where(visible, s, NEG_INF)
            e = jnp.exp(s - jnp.max(s, axis=-1, keepdims=True))
            return e, jnp.sum(e, axis=-1, keepdims=True)

        e1, l1 = softmax_parts(q[:, :DK], k[:, :DK])
        e2, l2 = softmax_parts(q[:, DK:], k[:, DK:])
        a = e1 * (1.0 / l1) - e2 * (lam / l2)
        o = jnp.dot(a.astype(BF16), v, preferred_element_type=F32)
        o_ref[i * tq:(i + 1) * tq, :] = _sub_layer_norm(o, sw).astype(o_ref.dtype)


def diff_attn_prompt(q, k, z_b, lam_vecs, subln_w, batch, seq, tq):
    vec = pl.BlockSpec((1, DK), lambda b, h: (0, 0))
    head = lambda off: pl.BlockSpec((seq, HEAD_W), lambda b, h: (b, h + off))
    return pl.pallas_call(
        functools.partial(_diff_attn_kernel, seq=seq, tq=tq),
        grid=(batch, N_HEADS),
        in_specs=[vec, vec, vec, vec, pl.BlockSpec((1, DV), lambda b, h: (0, 0)),
                  head(0), head(0), head(N_HEADS)],
        out_specs=head(0),
        out_shape=jax.ShapeDtypeStruct((batch * seq, QK_W), BF16),
        compiler_params=_cparams("parallel", "parallel"),
        name="diff_attn_prompt",
    )(*lam_vecs, subln_w.reshape(1, DV), q, k, z_b)


def _diff_attn_decode_kernel(pt_ref, lq1_ref, lk1_ref, lq2_ref, lk2_ref, sw_ref, q_ref, kn_ref, vn_ref, *rest,
                             pages_per_step):
    del pt_ref
    k_refs = rest[:pages_per_step]
    v_refs = rest[pages_per_step:2 * pages_per_step]
    o_ref, qs_ref, m_ref, l_ref, acc_ref = rest[2 * pages_per_step:]
    j = pl.program_id(1)
    page_rows = PAGE_SIZE * N_HEADS

    @pl.when(j == 0)
    def _():
        q = q_ref[...]
        lane = lax.broadcasted_iota(jnp.int32, q.shape, 1)
        qrows = jnp.concatenate([jnp.where(lane < DK, q, 0.0), jnp.where(lane >= DK, q, 0.0)], axis=0)
        qs_ref[...] = qrows.astype(BF16)
        k_new = kn_ref[...].astype(BF16).astype(F32)
        s_new = jnp.sum(qrows * jnp.concatenate([k_new, k_new], axis=0), axis=-1, keepdims=True)
        m_ref[...] = jnp.broadcast_to(s_new, m_ref.shape)
        l_ref[...] = jnp.ones(l_ref.shape, F32)
        acc_ref[...] = jnp.concatenate([vn_ref[...], vn_ref[...]], axis=0)

    qs = qs_ref[...]
    own = (lax.broadcasted_iota(jnp.int32, (2 * N_HEADS, page_rows), 1) % N_HEADS
           == lax.broadcasted_iota(jnp.int32, (2 * N_HEADS, page_rows), 0) % N_HEADS)
    s = jnp.concatenate(
        [jnp.where(own, lax.dot_general(qs, k_refs[r][...].astype(BF16), (((1,), (1,)), ((), ())),
                                        preferred_element_type=F32), NEG_INF)
         for r in range(pages_per_step)], axis=-1)
    m_old = m_ref[:, 0:1]
    m_new = jnp.maximum(m_old, jnp.max(s, axis=-1, keepdims=True))
    alpha = jnp.exp(m_old - m_new)
    p = jnp.exp(s - m_new)
    l_ref[...] = jnp.broadcast_to(alpha * l_ref[:, 0:1] + jnp.sum(p, axis=-1, keepdims=True), l_ref.shape)
    m_ref[...] = jnp.broadcast_to(m_new, m_ref.shape)
    pv = acc_ref[...] * alpha
    for r in range(pages_per_step):
        pv = pv + jnp.dot(p[:, r * page_rows:(r + 1) * page_rows].astype(BF16), v_refs[r][...].astype(BF16),
                          preferred_element_type=F32)
    acc_ref[...] = pv

    @pl.when(j == pl.num_programs(1) - 1)
    def _():
        lam = _lambda_value(lq1_ref, lk1_ref, lq2_ref, lk2_ref)
        o = acc_ref[...] / l_ref[:, 0:1]
        od = o[0:N_HEADS, :] - lam * o[N_HEADS:2 * N_HEADS, :]
        o_ref[...] = _sub_layer_norm(od, sw_ref[...])


def diff_attn_decode(q, k_new, v_new, cache_k, cache_v, page_table, lam_vecs, subln_w, pages_per_step):
    n_batch, n_pages = page_table.shape
    vec = pl.BlockSpec((1, DK), lambda b, j, pt: (0, 0))
    tok = lambda: pl.BlockSpec((None, N_HEADS, HEAD_W), lambda b, j, pt: (b, 0, 0))

    def page_spec(r):
        return pl.BlockSpec((None, PAGE_SIZE * N_HEADS, HEAD_W),
                            lambda b, j, pt: (pt[b, j * pages_per_step + r], 0, 0))

    grid_spec = pltpu.PrefetchScalarGridSpec(
        num_scalar_prefetch=1,
        grid=(n_batch, n_pages // pages_per_step),
        in_specs=[vec, vec, vec, vec, pl.BlockSpec((1, DV), lambda b, j, pt: (0, 0)), tok(), tok(), tok()]
        + [page_spec(r) for r in range(pages_per_step)] + [page_spec(r) for r in range(pages_per_step)],
        out_specs=tok(),
        scratch_shapes=[pltpu.VMEM((2 * N_HEADS, HEAD_W), BF16), pltpu.VMEM((2 * N_HEADS, LANES), F32),
                        pltpu.VMEM((2 * N_HEADS, LANES), F32), pltpu.VMEM((2 * N_HEADS, HEAD_W), F32)],
    )
    return pl.pallas_call(
        functools.partial(_diff_attn_decode_kernel, pages_per_step=pages_per_step),
        grid_spec=grid_spec,
        out_shape=jax.ShapeDtypeStruct((n_batch, N_HEADS, HEAD_W), F32),
        compiler_params=_cparams("parallel", "arbitrary"),
        name="diff_attn_decode",
    )(page_table, *lam_vecs, subln_w.reshape(1, DV), q, k_new, v_new,
      *([cache_k] * pages_per_step), *([cache_v] * pages_per_step))


def _mem_attn_kernel(q_ref, w_ref, mk_ref, mv_ref, o_ref):
    q = q_ref[...].astype(F32)
    ms = jnp.mean(q * q, axis=-1, keepdims=True)
    qn = (q * lax.rsqrt(ms + EPS) * w_ref[...] * (MEM_HD ** -0.5)).astype(BF16)
    s = lax.dot_general(qn, mk_ref[...].astype(BF16), (((1,), (1,)), ((), ())), preferred_element_type=F32)
    e = jnp.exp(s - jnp.max(s, axis=-1, keepdims=True))
    p = e * (1.0 / jnp.sum(e, axis=-1, keepdims=True))
    o_ref[...] = jnp.dot(p.astype(BF16), mv_ref[...].astype(BF16), preferred_element_type=F32).astype(o_ref.dtype)


def mem_attn(z_q, q_col, mq_norm_w, mk, mv, batch, seq, tq):
    nq = seq // tq
    return pl.pallas_call(
        _mem_attn_kernel,
        grid=(batch, MEM_HEADS, nq),
        in_specs=[pl.BlockSpec((tq, MEM_HD), lambda b, h, i: (b * nq + i, q_col + h)),
                  pl.BlockSpec((1, MEM_HD), lambda b, h, i: (0, 0)),
                  pl.BlockSpec((N_MEM, MEM_HD), lambda b, h, i: (b, h)),
                  pl.BlockSpec((N_MEM, MEM_HD), lambda b, h, i: (b, h))],
        out_specs=pl.BlockSpec((tq, MEM_HD), lambda b, h, i: (b * nq + i, h)),
        out_shape=jax.ShapeDtypeStruct((batch * seq, CROSS_W), BF16),
        compiler_params=_cparams("parallel", "parallel", "parallel"),
        name="mem_attn",
    )(z_q, mq_norm_w.reshape(1, MEM_HD), mk, mv)


def _merge_tail(zb, conv, od_ref, om_ref, g0_ref, g1_ref, g2_ref, wc_ref, wd_ref, wx_ref, o_ref):
    yc = jnp.dot((zb * conv).astype(BF16), wc_ref[...], preferred_element_type=F32)
    yd = jnp.dot(od_ref[...], wd_ref[...], preferred_element_type=F32)
    ym = jnp.dot(om_ref[...], wx_ref[...], preferred_element_type=F32)
    merged = (jax.nn.sigmoid(g0_ref[...].astype(F32)) * yc + jax.nn.sigmoid(g1_ref[...].astype(F32)) * yd
              + jax.nn.sigmoid(g2_ref[...].astype(F32)) * ym)
    o_ref[...] = merged.astype(o_ref.dtype)


def _merge_prompt_kernel(zb_ref, zc_ref, zx_ref, hc_ref, hx_ref, cw_ref, *rest, tiles_per_seq, tm):
    u = zc_ref[...].astype(F32) * zx_ref[...].astype(F32)
    first = pl.program_id(0) % tiles_per_seq == 0
    halo = jnp.where(first, 0.0, hc_ref[...].astype(F32) * hx_ref[...].astype(F32))
    h1 = halo[SAMPLE_ROWS - 1:SAMPLE_ROWS, :]
    h2 = halo[SAMPLE_ROWS - 2:SAMPLE_ROWS - 1, :]
    row = lax.broadcasted_iota(jnp.int32, u.shape, 0)
    u1 = jnp.where(row == 0, h1, pltpu.roll(u, 1, axis=0))
    u2 = jnp.where(row == 0, h2, jnp.where(row == 1, h1, pltpu.roll(u, 2, axis=0)))
    conv = cw_ref[0:1, :] * u2 + cw_ref[1:2, :] * u1 + cw_ref[2:3, :] * u
    _merge_tail(zb_ref[...].astype(F32), conv, *rest)


def _merge_sample_kernel(zb_ref, zc_ref, zx_ref, p2_ref, p1_ref, cw_ref, *rest):
    u = zc_ref[...].astype(F32) * zx_ref[...].astype(F32)
    conv = cw_ref[0:1, :] * p2_ref[...] + cw_ref[1:2, :] * p1_ref[...] + cw_ref[2:3, :] * u
    _merge_tail(zb_ref[...].astype(F32), conv, *rest)


def merge_branches(z_a, prev, conv_w, o_diff, o_mem, z_g, w_conv_out, w_diff_out, w_cross_out, tm, seq):
    m = z_a.shape[0]
    col = lambda c, w: pl.BlockSpec((tm, w), lambda i: (i, c))
    const = lambda shape: pl.BlockSpec(shape, lambda i: (0, 0))
    tail_specs = [col(0, D_CONV), col(0, CROSS_W), col(0, D_MODEL), col(1, D_MODEL), col(2, D_MODEL),
                  const((D_CONV, D_MODEL)), const((QK_W, D_MODEL)), const((CROSS_W, D_MODEL))]
    tail_args = (o_diff, o_mem, z_g, z_g, z_g, w_conv_out, w_diff_out, w_cross_out)
    if prev is None:
        blocks = tm // SAMPLE_ROWS
        halo = lambda c: pl.BlockSpec((SAMPLE_ROWS, D_CONV), lambda i: (jnp.maximum(i * blocks - 1, 0), c))
        kern = functools.partial(_merge_prompt_kernel, tiles_per_seq=seq // tm, tm=tm)
        head_specs = [col(0, D_CONV), col(1, D_CONV), col(2, D_CONV), halo(1), halo(2), const((3, D_CONV))]
        head_args = (z_a, z_a, z_a, z_a, z_a, conv_w)
    else:
        kern = _merge_sample_kernel
        head_specs = [col(0, D_CONV), col(1, D_CONV), col(2, D_CONV), col(0, D_CONV), col(0, D_CONV),
                      const((3, D_CONV))]
        head_args = (z_a, z_a, z_a, prev[0], prev[1], conv_w)
    return pl.pallas_call(
        kern,
        grid=(m // tm,),
        in_specs=head_specs + tail_specs,
        out_specs=col(0, D_MODEL),
        out_shape=jax.ShapeDtypeStruct((m, D_MODEL), BF16),
        compiler_params=_cparams("parallel"),
        name="merge_branches",
    )(*head_args, *tail_args)


def _conv_state_kernel(zc_ref, zx_ref, o_ref):
    u = zc_ref[...].astype(F32) * zx_ref[...].astype(F32)
    o_ref[...] = u[SAMPLE_ROWS - 2:SAMPLE_ROWS, :]


def conv_state_prompt(z_a, batch, seq):
    blocks = seq // SAMPLE_ROWS
    tail = lambda c: pl.BlockSpec((SAMPLE_ROWS, D_CONV), lambda b: (b * blocks + blocks - 1, c))
    return pl.pallas_call(
        _conv_state_kernel,
        grid=(batch,),
        in_specs=[tail(1), tail(2)],
        out_specs=pl.BlockSpec((None, 2, D_CONV), lambda b: (b, 0, 0)),
        out_shape=jax.ShapeDtypeStruct((batch, 2, D_CONV), F32),
        compiler_params=_cparams("parallel"),
        name="conv_state_prompt",
    )(z_a, z_a)


def _sample_u_kernel(zc_ref, zx_ref, o_ref):
    o_ref[...] = zc_ref[...].astype(F32) * zx_ref[...].astype(F32)


def sample_u(z_a):
    m = z_a.shape[0]
    blk = lambda c: pl.BlockSpec((m, D_CONV), lambda i: (0, c))
    return pl.pallas_call(
        _sample_u_kernel,
        grid=(1,),
        in_specs=[blk(1), blk(2)],
        out_specs=blk(0),
        out_shape=jax.ShapeDtypeStruct((m, D_CONV), F32),
        compiler_params=_cparams("arbitrary"),
        name="sample_u",
    )(z_a, z_a)


def _out_proj_kernel(x_ref, a_ref, w_ref, nw_ref, x1_ref, h_ref):
    x1 = x_ref[...] + jnp.dot(a_ref[...], w_ref[...], preferred_element_type=F32)
    x1_ref[...] = x1
    ms = jnp.mean(x1 * x1, axis=-1, keepdims=True)
    h_ref[...] = (x1 * lax.rsqrt(ms + EPS) * nw_ref[...]).astype(h_ref.dtype)


def out_proj(x, merged, w_o, norm2_w, tm):
    m, d = x.shape
    tm = min(tm, m)
    row = pl.BlockSpec((tm, d), lambda i: (i, 0))
    return pl.pallas_call(
        _out_proj_kernel,
        grid=(m // tm,),
        in_specs=[row, row, pl.BlockSpec((d, d), lambda i: (0, 0)), pl.BlockSpec((1, d), lambda i: (0, 0))],
        out_specs=[row, row],
        out_shape=[jax.ShapeDtypeStruct((m, d), F32), jax.ShapeDtypeStruct((m, d), BF16)],
        compiler_params=_cparams("parallel"),
        name="out_proj",
    )(x, merged, w_o, norm2_w.reshape(1, d))


def _top_values(x, count):
    vals = []
    for _ in range(count):
        m = jnp.max(x, axis=0, keepdims=True)
        vals.append(m)
        x = jnp.where(x == m, NEG_INF, x)
    return vals


def _peer_tables(qp_ref, sk_ref, s1_ref, e1_ref, s2_ref, e2_ref, t_ref):
    for h in range(PEER_HEADS):
        s = []
        for c in range(2):
            qc = qp_ref[:, (2 * h + c) * N_KEYS:(2 * h + c + 1) * N_KEYS]
            s.append(lax.dot_general(sk_ref[c], qc, (((1,), (1,)), ((), ())), preferred_element_type=F32))
        v1 = _top_values(s[0], PEER_TOPK)
        v2 = _top_values(s[1], PEER_TOPK)
        cands = []
        for a in range(PEER_TOPK):
            for b in range(PEER_TOPK // (a + 1)):
                cands.append(v1[a] + v2[b])
        cand = jnp.concatenate(cands, axis=0)
        top = _top_values(cand, PEER_TOPK)
        smax = v1[0] + v2[0]
        z = jnp.zeros_like(smax)
        for r in range(PEER_TOPK):
            z = z + jnp.exp(top[r] - smax)
        rows = slice(h * N_KEYS, (h + 1) * N_KEYS)
        s1_ref[rows, :] = s[0]
        s2_ref[rows, :] = s[1]
        e1_ref[rows, :] = jnp.exp(s[0] - v1[0]) / z
        e2_ref[rows, :] = jnp.exp(s[1] - v2[0])
        t_ref[h:h + 1, :] = top[PEER_TOPK - 1]


def _peer_chunk(c, n_chunks, ec, h_ref, u_ref, vt_ref, acc_ref, g_ref, w_prev_ref, w_next_ref,
                s1_ref, e1_ref, s2_ref, e2_ref, t_ref):
    chunks = ec // N_KEYS
    first = jnp.minimum(c, n_chunks - 1) * chunks
    tb = h_ref.shape[0]
    tw = min(tb, 2 * LANES)
    for tj in range(tb // tw):
        cols = slice(tj * tw, (tj + 1) * tw)
        acc_ref[:, cols] += jnp.dot(vt_ref[...], w_prev_ref[:, cols], preferred_element_type=F32)
        for il in range(chunks):
            gate = None
            for h in range(PEER_HEADS):
                row = h * N_KEYS + first + il
                s1 = s1_ref[pl.ds(row, 1), cols]
                e1 = e1_ref[pl.ds(row, 1), cols]
                keys = slice(h * N_KEYS, (h + 1) * N_KEYS)
                pair = s1 + s2_ref[keys, cols]
                g = jnp.where(pair >= t_ref[h:h + 1, cols], e1 * e2_ref[keys, cols], 0.0)
                gate = g if gate is None else gate + g
            g_ref[il * N_KEYS:(il + 1) * N_KEYS, cols] = gate
        a = lax.dot_general(u_ref[...], h_ref[cols, :], (((1,), (1,)), ((), ())), preferred_element_type=F32)
        act = 0.5 * a * (1.0 + lax.erf(a * INV_SQRT2))
        w_next_ref[:, cols] = (act * g_ref[:, cols]).astype(BF16)


def _peer_kernel(h_ref, qp_ref, sk_ref, x1_ref, u_ref, vt_ref, o_ref,
                 acc_ref, g_ref, wa_ref, wb_ref, s1_ref, e1_ref, s2_ref, e2_ref, t_ref, *, ec, n_chunks):
    c = pl.program_id(1)

    @pl.when(c == 0)
    def _():
        acc_ref[...] = jnp.zeros(acc_ref.shape, F32)
        wa_ref[...] = jnp.zeros(wa_ref.shape, BF16)
        _peer_tables(qp_ref, sk_ref, s1_ref, e1_ref, s2_ref, e2_ref, t_ref)

    step = functools.partial(_peer_chunk, c, n_chunks, ec, h_ref, u_ref, vt_ref, acc_ref, g_ref)
    tables = (s1_ref, e1_ref, s2_ref, e2_ref, t_ref)

    @pl.when(c % 2 == 0)
    def _():
        step(wa_ref, wb_ref, *tables)

    @pl.when(c % 2 == 1)
    def _():
        step(wb_ref, wa_ref, *tables)

    @pl.when(c == n_chunks)
    def _():
        o_ref[...] = x1_ref[...] + acc_ref[...].T


def peer_layer(x1, h2, qp, sub_keys, peer_u, peer_vt, tb, ec):
    m, d = x1.shape
    tb = min(tb, m)
    n_chunks = N_EXPERTS // ec
    tok = lambda: pl.BlockSpec((tb, d), lambda n, c: (n, 0))
    tab = lambda: pltpu.VMEM((PEER_HEADS * N_KEYS, tb), F32)
    wbuf = lambda: pltpu.VMEM((ec, tb), BF16)
    return pl.pallas_call(
        functools.partial(_peer_kernel, ec=ec, n_chunks=n_chunks),
        grid=(m // tb, n_chunks + 1),
        in_specs=[tok(), tok(), pl.BlockSpec((2, N_KEYS, D_KEY // 2), lambda n, c: (0, 0, 0)), tok(),
                  pl.BlockSpec((ec, d), lambda n, c: (jnp.minimum(c, n_chunks - 1), 0)),
                  pl.BlockSpec((d, ec), lambda n, c: (0, jnp.maximum(c - 1, 0)))],
        out_specs=tok(),
        out_shape=jax.ShapeDtypeStruct((m, d), F32),
        scratch_shapes=[pltpu.VMEM((d, tb), F32), pltpu.VMEM((ec, tb), F32), wbuf(), wbuf(),
                        tab(), tab(), tab(), tab(),
                        pltpu.VMEM((PEER_HEADS, tb), F32)],
        compiler_params=_cparams("parallel", "arbitrary"),
        name="peer_layer",
    )(h2, qp, sub_keys, x1, peer_u, peer_vt)


def _token_stack(x, rope_tabs, table_blocks, w, tiles):
    h = rmsnorm_cast(x, w["norm1_w"], tiles["norm"])
    w_in = w["w_in"]
    z_a = matmul(h, w_in, BF16, tiles["mm"], 1024, 0, KV_AT)
    z_b = matmul(h, w_in, F32, tiles["mm"], 1024, KV_AT, MEMQ_AT - KV_AT)
    z_m = matmul(h, w_in, BF16, tiles["mm"], 1024, MEMQ_AT, GATES_AT - MEMQ_AT)
    z_g = matmul(h, w_in, BF16, tiles["mm"], 1024, GATES_AT, 3 * D_MODEL)
    q, k, k_bf = qk_rope(z_a, z_b, w["q_norm_w"], w["k_norm_w"], rope_tabs, tiles["rope"], table_blocks)
    return z_a, z_b, z_m, z_g, q, k, k_bf


def _finish(x, merged, w, tiles):
    x1, h2 = out_proj(x, merged, w["w_o"], w["norm2_w"], tiles["out"])
    qp = matmul(h2, w["w_peer_q"], BF16, tiles["mm"], 1024)
    return peer_layer(x1, h2, qp, w["sub_keys"], w["peer_u"], w["peer_vt"], tiles["peer_tb"], tiles["peer_ec"])


def kernel(x_prompt, x_sample, mem_prompt, cache_k, cache_v, page_table, state_conv, cache_mem_k, cache_mem_v,
           norm1_w, w_in, conv_w, q_norm_w, k_norm_w, lambda_q1, lambda_k1, lambda_q2, lambda_k2, subln_w,
           mem_norm_w, w_mem_kv, mq_norm_w, mk_norm_w, w_conv_out, w_diff_out, w_cross_out, w_o, norm2_w,
           w_peer_q, sub_keys, peer_u, peer_v):
    batch, seq, d = x_prompt.shape
    n_dec = x_sample.shape[0]
    l = 0
    w = {
        "norm1_w": norm1_w[l], "norm2_w": norm2_w[l], "q_norm_w": q_norm_w[l], "k_norm_w": k_norm_w[l],
        "w_in": w_in[l].astype(BF16),
        "w_o": w_o[l].astype(BF16), "w_peer_q": w_peer_q[l].astype(BF16), "sub_keys": sub_keys[l].astype(BF16),
        "peer_u": peer_u[l].astype(BF16), "peer_vt": peer_v[l].T.astype(BF16),
    }
    w_conv_out_b = w_conv_out[l].astype(BF16)
    w_diff_out_b = w_diff_out[l].astype(BF16)
    w_cross_out_b = w_cross_out[l].astype(BF16)
    lam_vecs = tuple(v[l].reshape(1, DK) for v in (lambda_q1, lambda_k1, lambda_q2, lambda_k2))

    tiles_p = {"norm": 512, "mm": 1024, "rope": 512, "out": 512, "peer_tb": 512, "peer_ec": 512}
    xp = x_prompt.reshape(batch * seq, d)
    tabs_p = _rope_tables(jnp.arange(seq, dtype=jnp.int32))
    z_a, z_b, z_m, z_g, q, k, k_bf = _token_stack(xp, tabs_p, seq // tiles_p["rope"], w, tiles_p)
    h_mem = rmsnorm_cast(mem_prompt.reshape(batch * N_MEM, d), mem_norm_w[l], 512)
    kv_mem = matmul(h_mem, w_mem_kv[l].astype(BF16), F32, 1024, 1024)
    mk_p = group_norm(kv_mem, mk_norm_w[l], 512)
    mv_p = kv_mem[:, CROSS_W:]
    o_diff = diff_attn_prompt(q, k_bf, z_b, lam_vecs, subln_w[l], batch, seq, 256)
    o_mem = mem_attn(z_m, 0, mq_norm_w[l], mk_p, mv_p, batch, seq, 1024)
    merged = merge_branches(z_a, None, conv_w[l], o_diff, o_mem, z_g, w_conv_out_b, w_diff_out_b, w_cross_out_b,
                            256, seq)
    conv_p = conv_state_prompt(z_a, batch, seq)
    y_prompt = _finish(xp, merged, w, tiles_p).reshape(batch, seq, d)

    tiles_s = {"norm": 128, "mm": 128, "rope": 128, "out": 128, "peer_tb": 128, "peer_ec": 512}
    m_s = n_dec * SAMPLE_ROWS
    xs = jnp.pad(x_sample, ((0, 0), (0, SAMPLE_ROWS - x_sample.shape[1]), (0, 0))).reshape(m_s, d)
    tabs_s = _rope_tables(jnp.full((m_s,), PAST_LEN, jnp.int32))
    zs_a, zs_b, zs_m, zs_g, qs, ks, _ = _token_stack(xs, tabs_s, 1, w, tiles_s)
    pages = cache_k.shape[1]
    new_tok = lambda a: a.reshape(n_dec, SAMPLE_ROWS, N_HEADS, HEAD_W)[:, 0].astype(F32)
    os_heads = diff_attn_decode(new_tok(qs), new_tok(ks), new_tok(zs_b[:, QK_W:]),
                                cache_k[l].reshape(pages, PAGE_SIZE * N_HEADS, HEAD_W),
                                cache_v[l].reshape(pages, PAGE_SIZE * N_HEADS, HEAD_W),
                                page_table, lam_vecs, subln_w[l], 8)
    os_diff = jnp.pad(os_heads.reshape(n_dec, 1, QK_W).astype(BF16),
                      ((0, 0), (0, SAMPLE_ROWS - 1), (0, 0))).reshape(m_s, QK_W)
    os_mem = mem_attn(zs_m, 0, mq_norm_w[l], cache_mem_k[l].reshape(n_dec * N_MEM, CROSS_W),
                      cache_mem_v[l].reshape(n_dec * N_MEM, CROSS_W), n_dec, SAMPLE_ROWS, SAMPLE_ROWS)
    pad_state = lambda r: jnp.pad(state_conv[l][:, r:r + 1], ((0, 0), (0, SAMPLE_ROWS - 1), (0, 0))).reshape(m_s, D_CONV)
    merged_s = merge_branches(zs_a, (pad_state(0), pad_state(1)), conv_w[l], os_diff, os_mem, zs_g,
                              w_conv_out_b, w_diff_out_b, w_cross_out_b, 128, SAMPLE_ROWS)
    us = sample_u(zs_a).reshape(n_dec, SAMPLE_ROWS, D_CONV)[:, 0:1]
    conv_s = jnp.concatenate([state_conv[l][:, 1:2], us], axis=1)
    y_sample = _finish(xs, merged_s, w, tiles_s).reshape(n_dec, SAMPLE_ROWS, d)[:, 0:1]

    first = lambda a, n: a.reshape(n, SAMPLE_ROWS, N_HEADS, HEAD_W)[:, 0:1]
    return (y_prompt, y_sample,
            k.reshape(1, batch, seq, N_HEADS, HEAD_W), z_b[:, QK_W:].reshape(1, batch, seq, N_HEADS, DV),
            conv_p[None], mk_p.reshape(1, batch, N_MEM, MEM_HEADS, MEM_HD),
            mv_p.reshape(1, batch, N_MEM, MEM_HEADS, MEM_HD),
            first(ks, n_dec)[None], first(zs_b[:, QK_W:], n_dec)[None], conv_s[None])
```

```python
import functools
import math

import jax
import jax.numpy as jnp
import numpy as np
from jax import lax
from jax.experimental import pallas as pl
from jax.experimental.pallas import tpu as pltpu

F32 = jnp.float32
BF16 = jnp.bfloat16

D_MODEL = 2048
D_CONV = D_MODEL // 2
N_HEADS = 8
DK = D_MODEL // 32
DV = 2 * DK
HEAD_W = 2 * DK
QK_W = N_HEADS * 2 * DK
ROT_DIM = DK // 4
ROPE_THETA = 500000.0
N_MEM = 256
MEM_HEADS = 4
MEM_HD = D_MODEL // 8
CROSS_W = MEM_HEADS * MEM_HD
N_KEYS = 128
N_EXPERTS = N_KEYS * N_KEYS
PEER_HEADS = 8
PEER_TOPK = 16
D_KEY = 256
PAST_LEN = 16384
PAGE_SIZE = 128
EPS = 1e-6
NEG_INF = -1e30
KV_AT = 3 * D_CONV + QK_W
MEMQ_AT = KV_AT + 2 * QK_W
GATES_AT = MEMQ_AT + CROSS_W
LAM_INIT = 0.8 - 0.6 * math.exp(-0.3 * 0)
INV_SQRT2 = 1.0 / math.sqrt(2.0)

LANES = 128
SAMPLE_ROWS = 16
VMEM_LIMIT = 56 * 1024 * 1024


def _cparams(*sem, flags=None):
    return pltpu.CompilerParams(dimension_semantics=sem, vmem_limit_bytes=VMEM_LIMIT, flags=flags)


def _rmsnorm_kernel(x_ref, w_ref, o_ref):
    x = x_ref[...]
    ms = jnp.mean(x * x, axis=-1, keepdims=True)
    o_ref[...] = (x * lax.rsqrt(ms + EPS) * w_ref[...]).astype(o_ref.dtype)


def rmsnorm_cast(x, w, tm):
    m, d = x.shape
    return pl.pallas_call(
        _rmsnorm_kernel,
        grid=(m // tm,),
        in_specs=[pl.BlockSpec((tm, d), lambda i: (i, 0)), pl.BlockSpec((1, d), lambda i: (0, 0))],
        out_specs=pl.BlockSpec((tm, d), lambda i: (i, 0)),
        out_shape=jax.ShapeDtypeStruct((m, d), BF16),
        compiler_params=_cparams("parallel"),
        name="rmsnorm_cast",
    )(x, w.reshape(1, d))


def _matmul_kernel(a_ref, b_ref, o_ref):
    o_ref[...] = jnp.dot(a_ref[...], b_ref[...], preferred_element_type=F32).astype(o_ref.dtype)


def matmul(a, b, out_dtype, tm, tn, col0=0, n=None):
    m, k = a.shape
    n = b.shape[1] if n is None else n
    tm = min(tm, m)
    tn = min(tn, n)
    first = col0 // tn
    return pl.pallas_call(
        _matmul_kernel,
        grid=(m // tm, n // tn),
        in_specs=[pl.BlockSpec((tm, k), lambda i, j: (i, 0)), pl.BlockSpec((k, tn), lambda i, j: (0, j + first))],
        out_specs=pl.BlockSpec((tm, tn), lambda i, j: (i, j)),
        out_shape=jax.ShapeDtypeStruct((m, n), out_dtype),
        compiler_params=_cparams("parallel", "arbitrary"),
        name="matmul",
    )(a, b)


def _group_norm_kernel(x_ref, w_ref, o_ref):
    for g in range(MEM_HEADS):
        c = x_ref[:, g * MEM_HD:(g + 1) * MEM_HD]
        ms = jnp.mean(c * c, axis=-1, keepdims=True)
        o_ref[:, g * MEM_HD:(g + 1) * MEM_HD] = c * lax.rsqrt(ms + EPS) * w_ref[...]


def group_norm(x, w, tm):
    m = x.shape[0]
    return pl.pallas_call(
        _group_norm_kernel,
        grid=(m // tm,),
        in_specs=[pl.BlockSpec((tm, CROSS_W), lambda i: (i, 0)), pl.BlockSpec((1, MEM_HD), lambda i: (0, 0))],
        out_specs=pl.BlockSpec((tm, CROSS_W), lambda i: (i, 0)),
        out_shape=jax.ShapeDtypeStruct((m, CROSS_W), F32),
        compiler_params=_cparams("parallel"),
        name="group_norm",
    )(x, w.reshape(1, MEM_HD))


def _sub_head_norm_rope(x, w, ones_bd, ca, cm, cp):
    x2 = x * x
    hi = x2.astype(BF16)
    lo = (x2 - hi.astype(F32)).astype(BF16)
    ss = jnp.dot(hi, ones_bd, preferred_element_type=F32) + jnp.dot(lo, ones_bd, preferred_element_type=F32)
    y = x * lax.rsqrt(ss * (1.0 / DK) + EPS) * w
    half = ROT_DIM // 2
    y_up = pltpu.roll(y, LANES - half, axis=1)
    y_dn = pltpu.roll(y, half, axis=1)
    return y * ca + y_up * cm + y_dn * cp


def _qk_rope_kernel(zq_ref, zk_ref, qw_ref, kw_ref, bd_ref, ca_ref, cm_ref, cp_ref, q_ref, k_ref, kb_ref):
    ones_bd = bd_ref[...]
    ca, cm, cp = ca_ref[...], cm_ref[...], cp_ref[...]
    for c in range(QK_W // LANES):
        sl = slice(c * LANES, (c + 1) * LANES)
        q = _sub_head_norm_rope(zq_ref[:, sl].astype(F32), qw_ref[...], ones_bd, ca, cm, cp)
        q_ref[:, sl] = (q * (DK ** -0.5)).astype(BF16)
        k = _sub_head_norm_rope(zk_ref[:, sl], kw_ref[...], ones_bd, ca, cm, cp)
        k_ref[:, sl] = k
        kb_ref[:, sl] = k.astype(BF16)


def _rope_tables(pos):
    half = ROT_DIM // 2
    inv_freq = ROPE_THETA ** (-jnp.arange(half, dtype=F32) / half)
    ang = pos.astype(F32)[:, None] * inv_freq[None, :]
    cos, sin = jnp.cos(ang), jnp.sin(ang)
    t = pos.shape[0]
    ones = jnp.ones((t, DK - ROT_DIM), F32)
    zeros = jnp.zeros((t, DK - ROT_DIM), F32)
    zh = jnp.zeros((t, half), F32)
    ca = jnp.concatenate([cos, cos, ones], axis=-1)
    cm = jnp.concatenate([-sin, zh, zeros], axis=-1)
    cp = jnp.concatenate([zh, sin, zeros], axis=-1)
    return tuple(jnp.concatenate([a, a], axis=-1) for a in (ca, cm, cp))


def qk_rope(z_a, z_b, q_norm_w, k_norm_w, tables, tm, table_blocks):
    m = z_a.shape[0]
    qw = jnp.tile(q_norm_w.reshape(1, DK), (1, 2))
    kw = jnp.tile(k_norm_w.reshape(1, DK), (1, 2))
    grp = np.arange(LANES) // DK
    ones_bd = jnp.asarray(grp[:, None] == grp[None, :], BF16)
    row_spec = lambda col: pl.BlockSpec((tm, QK_W), lambda i: (i, col))
    tab_spec = pl.BlockSpec((tm, LANES), lambda i: (i % table_blocks, 0))
    vec_spec = pl.BlockSpec((1, LANES), lambda i: (0, 0))
    return pl.pallas_call(
        _qk_rope_kernel,
        grid=(m // tm,),
        in_specs=[row_spec(3), row_spec(0), vec_spec, vec_spec,
                  pl.BlockSpec((LANES, LANES), lambda i: (0, 0)), tab_spec, tab_spec, tab_spec],
        out_specs=[row_spec(0), row_spec(0), row_spec(0)],
        out_shape=[jax.ShapeDtypeStruct((m, QK_W), BF16), jax.ShapeDtypeStruct((m, QK_W), F32),
                   jax.ShapeDtypeStruct((m, QK_W), BF16)],
        compiler_params=_cparams("parallel"),
        name="qk_rope",
    )(z_a, z_b, qw, kw, ones_bd, *tables)


def _lambda_value(lq1_ref, lk1_ref, lq2_ref, lk2_ref):
    a = jnp.sum(lq1_ref[...] * lk1_ref[...], axis=-1, keepdims=True)
    b = jnp.sum(lq2_ref[...] * lk2_ref[...], axis=-1, keepdims=True)
    return jnp.exp(a) - jnp.exp(b) + LAM_INIT


def _sub_layer_norm(o, sw):
    ms = jnp.mean(o * o, axis=-1, keepdims=True)
    return o * lax.rsqrt(ms + EPS) * sw * (1.0 - LAM_INIT)


def _diff_attn_kernel(lq1_ref, lk1_ref, lq2_ref, lk2_ref, sw_ref, q_ref, k_ref, v_ref, o_ref, *, seq, tq):
    lam = _lambda_value(lq1_ref, lk1_ref, lq2_ref, lk2_ref)
    sw = sw_ref[...]
    for i in range(seq // tq):
        n_k = (i + 1) * tq
        q = q_ref[i * tq:(i + 1) * tq, :]
        k = k_ref[0:n_k, :]
        v = v_ref[0:n_k, :].astype(BF16)
        row = lax.broadcasted_iota(jnp.int32, (tq, n_k), 0) + i * tq
        col = lax.broadcasted_iota(jnp.int32, (tq, n_k), 1)
        visible = col <= row

        def softmax_parts(qc, kc):
            s = lax.dot_general(qc, kc, (((1,), (1,)), ((), ())), preferred_element_type=F32)
            s = jnp.where(visible, s, NEG_INF)
            e = jnp.exp(s - jnp.max(s, axis=-1, keepdims=True))
            return e, jnp.sum(e, axis=-1, keepdims=True)

        e1, l1 = softmax_parts(q[:, :DK], k[:, :DK])
        e2, l2 = softmax_parts(q[:, DK:], k[:, DK:])
        a = e1 * (1.0 / l1) - e2 * (lam / l2)
        o = jnp.dot(a.astype(BF16), v, preferred_element_type=F32)
        o_ref[i * tq:(i + 1) * tq, :] = _sub_layer_norm(o, sw).astype(o_ref.dtype)


def diff_attn_prompt(q, k, z_b, lam_vecs, subln_w, batch, seq, tq):
    vec = pl.BlockSpec((1, DK), lambda b, h: (0, 0))
    head = lambda off: pl.BlockSpec((seq, HEAD_W), lambda b, h: (b, h + off))
    return pl.pallas_call(
        functools.partial(_diff_attn_kernel, seq=seq, tq=tq),
        grid=(batch, N_HEADS),
        in_specs=[vec, vec, vec, vec, pl.BlockSpec((1, DV), lambda b, h: (0, 0)),
                  head(0), head(0), head(N_HEADS)],
        out_specs=head(0),
        out_shape=jax.ShapeDtypeStruct((batch * seq, QK_W), BF16),
        compiler_params=_cparams("parallel", "parallel"),
        name="diff_attn_prompt",
    )(*lam_vecs, subln_w.reshape(1, DV), q, k, z_b)


def _diff_attn_decode_kernel(pt_ref, lq1_ref, lk1_ref, lq2_ref, lk2_ref, sw_ref, q_ref, kn_ref, vn_ref, *rest,
                             pages_per_step):
    del pt_ref
    k_refs = rest[:pages_per_step]
    v_refs = rest[pages_per_step:2 * pages_per_step]
    o_ref, qs_ref, m_ref, l_ref, acc_ref = rest[2 * pages_per_step:]
    j = pl.program_id(1)
    page_rows = PAGE_SIZE * N_HEADS

    @pl.when(j == 0)
    def _():
        q = q_ref[...]
        lane = lax.broadcasted_iota(jnp.int32, q.shape, 1)
        qrows = jnp.concatenate([jnp.where(lane < DK, q, 0.0), jnp.where(lane >= DK, q, 0.0)], axis=0)
        qs_ref[...] = qrows.astype(BF16)
        k_new = kn_ref[...].astype(BF16).astype(F32)
        s_new = jnp.sum(qrows * jnp.concatenate([k_new, k_new], axis=0), axis=-1, keepdims=True)
        m_ref[...] = jnp.broadcast_to(s_new, m_ref.shape)
        l_ref[...] = jnp.ones(l_ref.shape, F32)
        acc_ref[...] = jnp.concatenate([vn_ref[...], vn_ref[...]], axis=0)

    qs = qs_ref[...]
    own = (lax.broadcasted_iota(jnp.int32, (2 * N_HEADS, page_rows), 1) % N_HEADS
           == lax.broadcasted_iota(jnp.int32, (2 * N_HEADS, page_rows), 0) % N_HEADS)
    s = jnp.concatenate(
        [jnp.where(own, lax.dot_general(qs, k_refs[r][...].astype(BF16), (((1,), (1,)), ((), ())),
                                        preferred_element_type=F32), NEG_INF)
         for r in range(pages_per_step)], axis=-1)
    m_old = m_ref[:, 0:1]
    m_new = jnp.maximum(m_old, jnp.max(s, axis=-1, keepdims=True))
    alpha = jnp.exp(m_old - m_new)
    p = jnp.exp(s - m_new)
    l_ref[...] = jnp.broadcast_to(alpha * l_ref[:, 0:1] + jnp.sum(p, axis=-1, keepdims=True), l_ref.shape)
    m_ref[...] = jnp.broadcast_to(m_new, m_ref.shape)
    pv = acc_ref[...] * alpha
    for r in range(pages_per_step):
        pv = pv + jnp.dot(p[:, r * page_rows:(r + 1) * page_rows].astype(BF16), v_refs[r][...].astype(BF16),
                          preferred_element_type=F32)
    acc_ref[...] = pv

    @pl.when(j == pl.num_programs(1) - 1)
    def _():
        lam = _lambda_value(lq1_ref, lk1_ref, lq2_ref, lk2_ref)
        o = acc_ref[...] / l_ref[:, 0:1]
        od = o[0:N_HEADS, :] - lam * o[N_HEADS:2 * N_HEADS, :]
        o_ref[...] = _sub_layer_norm(od, sw_ref[...])


def diff_attn_decode(q, k_new, v_new, cache_k, cache_v, page_table, lam_vecs, subln_w, pages_per_step):
    n_batch, n_pages = page_table.shape
    vec = pl.BlockSpec((1, DK), lambda b, j, pt: (0, 0))
    tok = lambda: pl.BlockSpec((None, N_HEADS, HEAD_W), lambda b, j, pt: (b, 0, 0))

    def page_spec(r):
        return pl.BlockSpec((None, PAGE_SIZE * N_HEADS, HEAD_W),
                            lambda b, j, pt: (pt[b, j * pages_per_step + r], 0, 0))

    grid_spec = pltpu.PrefetchScalarGridSpec(
        num_scalar_prefetch=1,
        grid=(n_batch, n_pages // pages_per_step),
        in_specs=[vec, vec, vec, vec, pl.BlockSpec((1, DV), lambda b, j, pt: (0, 0)), tok(), tok(), tok()]
        + [page_spec(r) for r in range(pages_per_step)] + [page_spec(r) for r in range(pages_per_step)],
        out_specs=tok(),
        scratch_shapes=[pltpu.VMEM((2 * N_HEADS, HEAD_W), BF16), pltpu.VMEM((2 * N_HEADS, LANES), F32),
                        pltpu.VMEM((2 * N_HEADS, LANES), F32), pltpu.VMEM((2 * N_HEADS, HEAD_W), F32)],
    )
    return pl.pallas_call(
        functools.partial(_diff_attn_decode_kernel, pages_per_step=pages_per_step),
        grid_spec=grid_spec,
        out_shape=jax.ShapeDtypeStruct((n_batch, N_HEADS, HEAD_W), F32),
        compiler_params=_cparams("parallel", "arbitrary"),
        name="diff_attn_decode",
    )(page_table, *lam_vecs, subln_w.reshape(1, DV), q, k_new, v_new,
      *([cache_k] * pages_per_step), *([cache_v] * pages_per_step))


def _mem_attn_kernel(q_ref, w_ref, mk_ref, mv_ref, o_ref):
    q = q_ref[...].astype(F32)
    ms = jnp.mean(q * q, axis=-1, keepdims=True)
    qn = (q * lax.rsqrt(ms + EPS) * w_ref[...] * (MEM_HD ** -0.5)).astype(BF16)
    s = lax.dot_general(qn, mk_ref[...].astype(BF16), (((1,), (1,)), ((), ())), preferred_element_type=F32)
    e = jnp.exp(s - jnp.max(s, axis=-1, keepdims=True))
    p = e * (1.0 / jnp.sum(e, axis=-1, keepdims=True))
    o_ref[...] = jnp.dot(p.astype(BF16), mv_ref[...].astype(BF16), preferred_element_type=F32).astype(o_ref.dtype)


def mem_attn(z_q, q_col, mq_norm_w, mk, mv, batch, seq, tq):
    nq = seq // tq
    return pl.pallas_call(
        _mem_attn_kernel,
        grid=(batch, MEM_HEADS, nq),
        in_specs=[pl.BlockSpec((tq, MEM_HD), lambda b, h, i: (b * nq + i, q_col + h)),
                  pl.BlockSpec((1, MEM_HD), lambda b, h, i: (0, 0)),
                  pl.BlockSpec((N_MEM, MEM_HD), lambda b, h, i: (b, h)),
                  pl.BlockSpec((N_MEM, MEM_HD), lambda b, h, i: (b, h))],
        out_specs=pl.BlockSpec((tq, MEM_HD), lambda b, h, i: (b * nq + i, h)),
        out_shape=jax.ShapeDtypeStruct((batch * seq, CROSS_W), BF16),
        compiler_params=_cparams("parallel", "parallel", "parallel"),
        name="mem_attn",
    )(z_q, mq_norm_w.reshape(1, MEM_HD), mk, mv)


def _merge_tail(zb, conv, od_ref, om_ref, g0_ref, g1_ref, g2_ref, wc_ref, wd_ref, wx_ref, o_ref):
    yc = jnp.dot((zb * conv).astype(BF16), wc_ref[...], preferred_element_type=F32)
    yd = jnp.dot(od_ref[...], wd_ref[...], preferred_element_type=F32)
    ym = jnp.dot(om_ref[...], wx_ref[...], preferred_element_type=F32)
    merged = (jax.nn.sigmoid(g0_ref[...].astype(F32)) * yc + jax.nn.sigmoid(g1_ref[...].astype(F32)) * yd
              + jax.nn.sigmoid(g2_ref[...].astype(F32)) * ym)
    o_ref[...] = merged.astype(o_ref.dtype)


def _merge_prompt_kernel(zb_ref, zc_ref, zx_ref, hc_ref, hx_ref, cw_ref, *rest, tiles_per_seq, tm):
    u = zc_ref[...].astype(F32) * zx_ref[...].astype(F32)
    first = pl.program_id(0) % tiles_per_seq == 0
    halo = jnp.where(first, 0.0, hc_ref[...].astype(F32) * hx_ref[...].astype(F32))
    h1 = halo[SAMPLE_ROWS - 1:SAMPLE_ROWS, :]
    h2 = halo[SAMPLE_ROWS - 2:SAMPLE_ROWS - 1, :]
    row = lax.broadcasted_iota(jnp.int32, u.shape, 0)
    u1 = jnp.where(row == 0, h1, pltpu.roll(u, 1, axis=0))
    u2 = jnp.where(row == 0, h2, jnp.where(row == 1, h1, pltpu.roll(u, 2, axis=0)))
    conv = cw_ref[0:1, :] * u2 + cw_ref[1:2, :] * u1 + cw_ref[2:3, :] * u
    _merge_tail(zb_ref[...].astype(F32), conv, *rest)


def _merge_sample_kernel(zb_ref, zc_ref, zx_ref, p2_ref, p1_ref, cw_ref, *rest):
    u = zc_ref[...].astype(F32) * zx_ref[...].astype(F32)
    conv = cw_ref[0:1, :] * p2_ref[...] + cw_ref[1:2, :] * p1_ref[...] + cw_ref[2:3, :] * u
    _merge_tail(zb_ref[...].astype(F32), conv, *rest)


def merge_branches(z_a, prev, conv_w, o_diff, o_mem, z_g, w_conv_out, w_diff_out, w_cross_out, tm, seq):
    m = z_a.shape[0]
    col = lambda c, w: pl.BlockSpec((tm, w), lambda i: (i, c))
    const = lambda shape: pl.BlockSpec(shape, lambda i: (0, 0))
    tail_specs = [col(0, D_CONV), col(0, CROSS_W), col(0, D_MODEL), col(1, D_MODEL), col(2, D_MODEL),
                  const((D_CONV, D_MODEL)), const((QK_W, D_MODEL)), const((CROSS_W, D_MODEL))]
    tail_args = (o_diff, o_mem, z_g, z_g, z_g, w_conv_out, w_diff_out, w_cross_out)
    if prev is None:
        blocks = tm // SAMPLE_ROWS
        halo = lambda c: pl.BlockSpec((SAMPLE_ROWS, D_CONV), lambda i: (jnp.maximum(i * blocks - 1, 0), c))
        kern = functools.partial(_merge_prompt_kernel, tiles_per_seq=seq // tm, tm=tm)
        head_specs = [col(0, D_CONV), col(1, D_CONV), col(2, D_CONV), halo(1), halo(2), const((3, D_CONV))]
        head_args = (z_a, z_a, z_a, z_a, z_a, conv_w)
    else:
        kern = _merge_sample_kernel
        head_specs = [col(0, D_CONV), col(1, D_CONV), col(2, D_CONV), col(0, D_CONV), col(0, D_CONV),
                      const((3, D_CONV))]
        head_args = (z_a, z_a, z_a, prev[0], prev[1], conv_w)
    return pl.pallas_call(
        kern,
        grid=(m // tm,),
        in_specs=head_specs + tail_specs,
        out_specs=col(0, D_MODEL),
        out_shape=jax.ShapeDtypeStruct((m, D_MODEL), BF16),
        compiler_params=_cparams("parallel"),
        name="merge_branches",
    )(*head_args, *tail_args)


def _conv_state_kernel(zc_ref, zx_ref, o_ref):
    u = zc_ref[...].astype(F32) * zx_ref[...].astype(F32)
    o_ref[...] = u[SAMPLE_ROWS - 2:SAMPLE_ROWS, :]


def conv_state_prompt(z_a, batch, seq):
    blocks = seq // SAMPLE_ROWS
    tail = lambda c: pl.BlockSpec((SAMPLE_ROWS, D_CONV), lambda b: (b * blocks + blocks - 1, c))
    return pl.pallas_call(
        _conv_state_kernel,
        grid=(batch,),
        in_specs=[tail(1), tail(2)],
        out_specs=pl.BlockSpec((None, 2, D_CONV), lambda b: (b, 0, 0)),
        out_shape=jax.ShapeDtypeStruct((batch, 2, D_CONV), F32),
        compiler_params=_cparams("parallel"),
        name="conv_state_prompt",
    )(z_a, z_a)


def _sample_u_kernel(zc_ref, zx_ref, o_ref):
    o_ref[...] = zc_ref[...].astype(F32) * zx_ref[...].astype(F32)


def sample_u(z_a):
    m = z_a.shape[0]
    blk = lambda c: pl.BlockSpec((m, D_CONV), lambda i: (0, c))
    return pl.pallas_call(
        _sample_u_kernel,
        grid=(1,),
        in_specs=[blk(1), blk(2)],
        out_specs=blk(0),
        out_shape=jax.ShapeDtypeStruct((m, D_CONV), F32),
        compiler_params=_cparams("arbitrary"),
        name="sample_u",
    )(z_a, z_a)


def _out_proj_kernel(x_ref, a_ref, w_ref, nw_ref, x1_ref, h_ref):
    x1 = x_ref[...] + jnp.dot(a_ref[...], w_ref[...], preferred_element_type=F32)
    x1_ref[...] = x1
    ms = jnp.mean(x1 * x1, axis=-1, keepdims=True)
    h_ref[...] = (x1 * lax.rsqrt(ms + EPS) * nw_ref[...]).astype(h_ref.dtype)


def out_proj(x, merged, w_o, norm2_w, tm):
    m, d = x.shape
    tm = min(tm, m)
    row = pl.BlockSpec((tm, d), lambda i: (i, 0))
    return pl.pallas_call(
        _out_proj_kernel,
        grid=(m // tm,),
        in_specs=[row, row, pl.BlockSpec((d, d), lambda i: (0, 0)), pl.BlockSpec((1, d), lambda i: (0, 0))],
        out_specs=[row, row],
        out_shape=[jax.ShapeDtypeStruct((m, d), F32), jax.ShapeDtypeStruct((m, d), BF16)],
        compiler_params=_cparams("parallel"),
        name="out_proj",
    )(x, merged, w_o, norm2_w.reshape(1, d))


def _top_values(x, count):
    vals = []
    for _ in range(count):
        m = jnp.max(x, axis=0, keepdims=True)
        vals.append(m)
        x = jnp.where(x == m, NEG_INF, x)
    return vals


def _peer_tables(qp_ref, sk_ref, s1_ref, e1_ref, s2_ref, e2_ref, t_ref):
    for h in range(PEER_HEADS):
        s = []
        for c in range(2):
            qc = qp_ref[:, (2 * h + c) * N_KEYS:(2 * h + c + 1) * N_KEYS]
            s.append(lax.dot_general(sk_ref[c], qc, (((1,), (1,)), ((), ())), preferred_element_type=F32))
        v1 = _top_values(s[0], PEER_TOPK)
        v2 = _top_values(s[1], PEER_TOPK)
        cands = []
        for a in range(PEER_TOPK):
            for b in range(PEER_TOPK // (a + 1)):
                cands.append(v1[a] + v2[b])
        cand = jnp.concatenate(cands, axis=0)
        top = _top_values(cand, PEER_TOPK)
        smax = v1[0] + v2[0]
        z = jnp.zeros_like(smax)
        for r in range(PEER_TOPK):
            z = z + jnp.exp(top[r] - smax)
        rows = slice(h * N_KEYS, (h + 1) * N_KEYS)
        s1_ref[rows, :] = s[0]
        s2_ref[rows, :] = s[1]
        e1_ref[rows, :] = jnp.exp(s[0] - v1[0]) / z
        e2_ref[rows, :] = jnp.exp(s[1] - v2[0])
        t_ref[h:h + 1, :] = top[PEER_TOPK - 1]


def _peer_chunk(c, n_chunks, ec, h_ref, u_ref, v_ref, acc_ref, g_ref, w_prev_ref, w_next_ref,
                s1_ref, e1_ref, s2_ref, e2_ref, t_ref):
    chunks = ec // N_KEYS
    first = jnp.minimum(c, n_chunks - 1) * chunks
    tb = h_ref.shape[0]
    tw = min(tb, 2 * LANES)
    u_bf = u_ref[...].astype(BF16)
    v_bf = v_ref[...].astype(BF16)
    for tj in range(tb // tw):
        cols = slice(tj * tw, (tj + 1) * tw)
        acc_ref[:, cols] += lax.dot_general(v_bf, w_prev_ref[:, cols], (((0,), (0,)), ((), ())),
                                            preferred_element_type=F32)
        for il in range(chunks):
            gate = None
            for h in range(PEER_HEADS):
                row = h * N_KEYS + first + il
                s1 = s1_ref[pl.ds(row, 1), cols]
                e1 = e1_ref[pl.ds(row, 1), cols]
                keys = slice(h * N_KEYS, (h + 1) * N_KEYS)
                pair = s1 + s2_ref[keys, cols]
                g = jnp.where(pair >= t_ref[h:h + 1, cols], e1 * e2_ref[keys, cols], 0.0)
                gate = g if gate is None else gate + g
            g_ref[il * N_KEYS:(il + 1) * N_KEYS, cols] = gate
        a = lax.dot_general(u_bf, h_ref[cols, :], (((1,), (1,)), ((), ())), preferred_element_type=F32)
        act = 0.5 * a * (1.0 + lax.erf(a * INV_SQRT2))
        w_next_ref[:, cols] = (act * g_ref[:, cols]).astype(BF16)


def _peer_kernel(h_ref, qp_ref, sk_ref, x1_ref, u_ref, v_ref, o_ref,
                 acc_ref, g_ref, wa_ref, wb_ref, s1_ref, e1_ref, s2_ref, e2_ref, t_ref, *, ec, n_chunks):
    c = pl.program_id(1)

    @pl.when(c == 0)
    def _():
        acc_ref[...] = jnp.zeros(acc_ref.shape, F32)
        wa_ref[...] = jnp.zeros(wa_ref.shape, BF16)
        _peer_tables(qp_ref, sk_ref, s1_ref, e1_ref, s2_ref, e2_ref, t_ref)

    step = functools.partial(_peer_chunk, c, n_chunks, ec, h_ref, u_ref, v_ref, acc_ref, g_ref)
    tables = (s1_ref, e1_ref, s2_ref, e2_ref, t_ref)

    @pl.when(c % 2 == 0)
    def _():
        step(wa_ref, wb_ref, *tables)

    @pl.when(c % 2 == 1)
    def _():
        step(wb_ref, wa_ref, *tables)

    @pl.when(c == n_chunks)
    def _():
        o_ref[...] = x1_ref[...] + acc_ref[...].T


def peer_layer(x1, h2, qp, sub_keys, peer_u, peer_v, tb, ec):
    m, d = x1.shape
    tb = min(tb, m)
    n_chunks = N_EXPERTS // ec
    tok = lambda: pl.BlockSpec((tb, d), lambda n, c: (n, 0))
    tok_once = lambda: pl.BlockSpec((tb, d), lambda n, c: (n, 0), pipeline_mode=pl.Buffered(1))
    tab = lambda: pltpu.VMEM((PEER_HEADS * N_KEYS, tb), F32)
    wbuf = lambda: pltpu.VMEM((ec, tb), BF16)
    return pl.pallas_call(
        functools.partial(_peer_kernel, ec=ec, n_chunks=n_chunks),
        grid=(m // tb, n_chunks + 1),
        in_specs=[tok(), tok(), pl.BlockSpec((2, N_KEYS, D_KEY // 2), lambda n, c: (0, 0, 0)), tok_once(),
                  pl.BlockSpec((ec, d), lambda n, c: (jnp.minimum(c, n_chunks - 1), 0)),
                  pl.BlockSpec((ec, d), lambda n, c: (jnp.maximum(c - 1, 0), 0))],
        out_specs=tok(),
        out_shape=jax.ShapeDtypeStruct((m, d), F32),
        scratch_shapes=[pltpu.VMEM((d, tb), F32), pltpu.VMEM((ec, tb), F32), wbuf(), wbuf(),
                        tab(), tab(), tab(), tab(),
                        pltpu.VMEM((PEER_HEADS, tb), F32)],
        compiler_params=_cparams("parallel", "arbitrary"),
        name="peer_layer",
    )(h2, qp, sub_keys, x1, peer_u, peer_v)


def _token_stack(x, rope_tabs, table_blocks, w, tiles):
    h = rmsnorm_cast(x, w["norm1_w"], tiles["norm"])
    w_in = w["w_in"]
    z_a = matmul(h, w_in, BF16, tiles["mm"], 1024, 0, KV_AT)
    z_b = matmul(h, w_in, F32, tiles["mm"], 1024, KV_AT, MEMQ_AT - KV_AT)
    z_m = matmul(h, w_in, BF16, tiles["mm"], 1024, MEMQ_AT, GATES_AT - MEMQ_AT)
    z_g = matmul(h, w_in, BF16, tiles["mm"], 1024, GATES_AT, 3 * D_MODEL)
    q, k, k_bf = qk_rope(z_a, z_b, w["q_norm_w"], w["k_norm_w"], rope_tabs, tiles["rope"], table_blocks)
    return z_a, z_b, z_m, z_g, q, k, k_bf


def _finish(x, merged, w, tiles):
    x1, h2 = out_proj(x, merged, w["w_o"], w["norm2_w"], tiles["out"])
    qp = matmul(h2, w["w_peer_q"], BF16, tiles["mm"], 1024)
    return peer_layer(x1, h2, qp, w["sub_keys"], w["peer_u"], w["peer_v"], tiles["peer_tb"], tiles["peer_ec"])


def kernel(x_prompt, x_sample, mem_prompt, cache_k, cache_v, page_table, state_conv, cache_mem_k, cache_mem_v,
           norm1_w, w_in, conv_w, q_norm_w, k_norm_w, lambda_q1, lambda_k1, lambda_q2, lambda_k2, subln_w,
           mem_norm_w, w_mem_kv, mq_norm_w, mk_norm_w, w_conv_out, w_diff_out, w_cross_out, w_o, norm2_w,
           w_peer_q, sub_keys, peer_u, peer_v):
    batch, seq, d = x_prompt.shape
    n_dec = x_sample.shape[0]
    l = 0
    w = {
        "norm1_w": norm1_w[l], "norm2_w": norm2_w[l], "q_norm_w": q_norm_w[l], "k_norm_w": k_norm_w[l],
        "w_in": w_in[l].astype(BF16),
        "w_o": w_o[l].astype(BF16), "w_peer_q": w_peer_q[l].astype(BF16), "sub_keys": sub_keys[l].astype(BF16),
        "peer_u": peer_u[l], "peer_v": peer_v[l],
    }
    w_conv_out_b = w_conv_out[l].astype(BF16)
    w_diff_out_b = w_diff_out[l].astype(BF16)
    w_cross_out_b = w_cross_out[l].astype(BF16)
    lam_vecs = tuple(v[l].reshape(1, DK) for v in (lambda_q1, lambda_k1, lambda_q2, lambda_k2))

    tiles_p = {"norm": 512, "mm": 1024, "rope": 512, "out": 512, "peer_tb": 512, "peer_ec": 512}
    xp = x_prompt.reshape(batch * seq, d)
    tabs_p = _rope_tables(jnp.arange(seq, dtype=jnp.int32))
    z_a, z_b, z_m, z_g, q, k, k_bf = _token_stack(xp, tabs_p, seq // tiles_p["rope"], w, tiles_p)
    h_mem = rmsnorm_cast(mem_prompt.reshape(batch * N_MEM, d), mem_norm_w[l], 512)
    kv_mem = matmul(h_mem, w_mem_kv[l].astype(BF16), F32, 1024, 1024)
    mk_p = group_norm(kv_mem, mk_norm_w[l], 512)
    mv_p = kv_mem[:, CROSS_W:]
    o_diff = diff_attn_prompt(q, k_bf, z_b, lam_vecs, subln_w[l], batch, seq, 256)
    o_mem = mem_attn(z_m, 0, mq_norm_w[l], mk_p, mv_p, batch, seq, 1024)
    merged = merge_branches(z_a, None, conv_w[l], o_diff, o_mem, z_g, w_conv_out_b, w_diff_out_b, w_cross_out_b,
                            256, seq)
    conv_p = conv_state_prompt(z_a, batch, seq)
    y_prompt = _finish(xp, merged, w, tiles_p).reshape(batch, seq, d)

    tiles_s = {"norm": 128, "mm": 128, "rope": 128, "out": 128, "peer_tb": 128, "peer_ec": 512}
    m_s = n_dec * SAMPLE_ROWS
    xs = jnp.pad(x_sample, ((0, 0), (0, SAMPLE_ROWS - x_sample.shape[1]), (0, 0))).reshape(m_s, d)
    tabs_s = _rope_tables(jnp.full((m_s,), PAST_LEN, jnp.int32))
    zs_a, zs_b, zs_m, zs_g, qs, ks, _ = _token_stack(xs, tabs_s, 1, w, tiles_s)
    pages = cache_k.shape[1]
    new_tok = lambda a: a.reshape(n_dec, SAMPLE_ROWS, N_HEADS, HEAD_W)[:, 0].astype(F32)
    os_heads = diff_attn_decode(new_tok(qs), new_tok(ks), new_tok(zs_b[:, QK_W:]),
                                cache_k[l].reshape(pages, PAGE_SIZE * N_HEADS, HEAD_W),
                                cache_v[l].reshape(pages, PAGE_SIZE * N_HEADS, HEAD_W),
                                page_table, lam_vecs, subln_w[l], 8)
    os_diff = jnp.pad(os_heads.reshape(n_dec, 1, QK_W).astype(BF16),
                      ((0, 0), (0, SAMPLE_ROWS - 1), (0, 0))).reshape(m_s, QK_W)
    os_mem = mem_attn(zs_m, 0, mq_norm_w[l], cache_mem_k[l].reshape(n_dec * N_MEM, CROSS_W),
                      cache_mem_v[l].reshape(n_dec * N_MEM, CROSS_W), n_dec, SAMPLE_ROWS, SAMPLE_ROWS)
    pad_state = lambda r: jnp.pad(state_conv[l][:, r:r + 1], ((0, 0), (0, SAMPLE_ROWS - 1), (0, 0))).reshape(m_s, D_CONV)
    merged_s = merge_branches(zs_a, (pad_state(0), pad_state(1)), conv_w[l], os_diff, os_mem, zs_g,
                              w_conv_out_b, w_diff_out_b, w_cross_out_b, 128, SAMPLE_ROWS)
    us = sample_u(zs_a).reshape(n_dec, SAMPLE_ROWS, D_CONV)[:, 0:1]
    conv_s = jnp.concatenate([state_conv[l][:, 1:2], us], axis=1)
    y_sample = _finish(xs, merged_s, w, tiles_s).reshape(n_dec, SAMPLE_ROWS, d)[:, 0:1]

    first = lambda a, n: a.reshape(n, SAMPLE_ROWS, N_HEADS, HEAD_W)[:, 0:1]
    return (y_prompt, y_sample,
            k.reshape(1, batch, seq, N_HEADS, HEAD_W), z_b[:, QK_W:].reshape(1, batch, seq, N_HEADS, DV),
            conv_p[None], mk_p.reshape(1, batch, N_MEM, MEM_HEADS, MEM_HD),
            mv_p.reshape(1, batch, N_MEM, MEM_HEADS, MEM_HD),
            first(ks, n_dec)[None], first(zs_b[:, QK_W:], n_dec)[None], conv_s[None])
```

```python
import functools
import math

import jax
import jax.numpy as jnp
import numpy as np
from jax import lax
from jax.experimental import pallas as pl
from jax.experimental.pallas import tpu as pltpu

F32 = jnp.float32
BF16 = jnp.bfloat16

D_MODEL = 2048
D_CONV = D_MODEL // 2
N_HEADS = 8
DK = D_MODEL // 32
DV = 2 * DK
HEAD_W = 2 * DK
QK_W = N_HEADS * 2 * DK
ROT_DIM = DK // 4
ROPE_THETA = 500000.0
N_MEM = 256
MEM_HEADS = 4
MEM_HD = D_MODEL // 8
CROSS_W = MEM_HEADS * MEM_HD
N_KEYS = 128
N_EXPERTS = N_KEYS * N_KEYS
PEER_HEADS = 8
PEER_TOPK = 16
D_KEY = 256
PAST_LEN = 16384
PAGE_SIZE = 128
EPS = 1e-6
NEG_INF = -1e30
KV_AT = 3 * D_CONV + QK_W
MEMQ_AT = KV_AT + 2 * QK_W
GATES_AT = MEMQ_AT + CROSS_W
LAM_INIT = 0.8 - 0.6 * math.exp(-0.3 * 0)
INV_SQRT2 = 1.0 / math.sqrt(2.0)

LANES = 128
SAMPLE_ROWS = 16
VMEM_LIMIT = 56 * 1024 * 1024


def _cparams(*sem, flags=None):
    return pltpu.CompilerParams(dimension_semantics=sem, vmem_limit_bytes=VMEM_LIMIT, flags=flags)


def _rmsnorm_kernel(x_ref, w_ref, o_ref):
    x = x_ref[...]
    ms = jnp.mean(x * x, axis=-1, keepdims=True)
    o_ref[...] = (x * lax.rsqrt(ms + EPS) * w_ref[...]).astype(o_ref.dtype)


def rmsnorm_cast(x, w, tm):
    m, d = x.shape
    return pl.pallas_call(
        _rmsnorm_kernel,
        grid=(m // tm,),
        in_specs=[pl.BlockSpec((tm, d), lambda i: (i, 0)), pl.BlockSpec((1, d), lambda i: (0, 0))],
        out_specs=pl.BlockSpec((tm, d), lambda i: (i, 0)),
        out_shape=jax.ShapeDtypeStruct((m, d), BF16),
        compiler_params=_cparams("parallel"),
        name="rmsnorm_cast",
    )(x, w.reshape(1, d))


def _matmul_kernel(a_ref, b_ref, o_ref):
    o_ref[...] = jnp.dot(a_ref[...], b_ref[...], preferred_element_type=F32).astype(o_ref.dtype)


def matmul(a, b, out_dtype, tm, tn, col0=0, n=None):
    m, k = a.shape
    n = b.shape[1] if n is None else n
    tm = min(tm, m)
    tn = min(tn, n)
    first = col0 // tn
    return pl.pallas_call(
        _matmul_kernel,
        grid=(m // tm, n // tn),
        in_specs=[pl.BlockSpec((tm, k), lambda i, j: (i, 0)), pl.BlockSpec((k, tn), lambda i, j: (0, j + first))],
        out_specs=pl.BlockSpec((tm, tn), lambda i, j: (i, j)),
        out_shape=jax.ShapeDtypeStruct((m, n), out_dtype),
        compiler_params=_cparams("parallel", "arbitrary"),
        name="matmul",
    )(a, b)


def _group_norm_kernel(x_ref, w_ref, o_ref):
    for g in range(MEM_HEADS):
        c = x_ref[:, g * MEM_HD:(g + 1) * MEM_HD]
        ms = jnp.mean(c * c, axis=-1, keepdims=True)
        o_ref[:, g * MEM_HD:(g + 1) * MEM_HD] = c * lax.rsqrt(ms + EPS) * w_ref[...]


def group_norm(x, w, tm):
    m = x.shape[0]
    return pl.pallas_call(
        _group_norm_kernel,
        grid=(m // tm,),
        in_specs=[pl.BlockSpec((tm, CROSS_W), lambda i: (i, 0)), pl.BlockSpec((1, MEM_HD), lambda i: (0, 0))],
        out_specs=pl.BlockSpec((tm, CROSS_W), lambda i: (i, 0)),
        out_shape=jax.ShapeDtypeStruct((m, CROSS_W), F32),
        compiler_params=_cparams("parallel"),
        name="group_norm",
    )(x, w.reshape(1, MEM_HD))


def _sub_head_norm_rope(x, w, ones_bd, ca, cm, cp):
    x2 = x * x
    hi = x2.astype(BF16)
    lo = (x2 - hi.astype(F32)).astype(BF16)
    ss = jnp.dot(hi, ones_bd, preferred_element_type=F32) + jnp.dot(lo, ones_bd, preferred_element_type=F32)
    y = x * lax.rsqrt(ss * (1.0 / DK) + EPS) * w
    half = ROT_DIM // 2
    y_up = pltpu.roll(y, LANES - half, axis=1)
    y_dn = pltpu.roll(y, half, axis=1)
    return y * ca + y_up * cm + y_dn * cp


def _qk_rope_kernel(zq_ref, zk_ref, qw_ref, kw_ref, bd_ref, ca_ref, cm_ref, cp_ref, q_ref, k_ref, kb_ref):
    ones_bd = bd_ref[...]
    ca, cm, cp = ca_ref[...], cm_ref[...], cp_ref[...]
    for c in range(QK_W // LANES):
        sl = slice(c * LANES, (c + 1) * LANES)
        q = _sub_head_norm_rope(zq_ref[:, sl].astype(F32), qw_ref[...], ones_bd, ca, cm, cp)
        q_ref[:, sl] = (q * (DK ** -0.5)).astype(BF16)
        k = _sub_head_norm_rope(zk_ref[:, sl], kw_ref[...], ones_bd, ca, cm, cp)
        k_ref[:, sl] = k
        kb_ref[:, sl] = k.astype(BF16)


def _rope_tables(pos):
    half = ROT_DIM // 2
    inv_freq = ROPE_THETA ** (-jnp.arange(half, dtype=F32) / half)
    ang = pos.astype(F32)[:, None] * inv_freq[None, :]
    cos, sin = jnp.cos(ang), jnp.sin(ang)
    t = pos.shape[0]
    ones = jnp.ones((t, DK - ROT_DIM), F32)
    zeros = jnp.zeros((t, DK - ROT_DIM), F32)
    zh = jnp.zeros((t, half), F32)
    ca = jnp.concatenate([cos, cos, ones], axis=-1)
    cm = jnp.concatenate([-sin, zh, zeros], axis=-1)
    cp = jnp.concatenate([zh, sin, zeros], axis=-1)
    return tuple(jnp.concatenate([a, a], axis=-1) for a in (ca, cm, cp))


def qk_rope(z_a, z_b, q_norm_w, k_norm_w, tables, tm, table_blocks):
    m = z_a.shape[0]
    qw = jnp.tile(q_norm_w.reshape(1, DK), (1, 2))
    kw = jnp.tile(k_norm_w.reshape(1, DK), (1, 2))
    grp = np.arange(LANES) // DK
    ones_bd = jnp.asarray(grp[:, None] == grp[None, :], BF16)
    row_spec = lambda col: pl.BlockSpec((tm, QK_W), lambda i: (i, col))
    tab_spec = pl.BlockSpec((tm, LANES), lambda i: (i % table_blocks, 0))
    vec_spec = pl.BlockSpec((1, LANES), lambda i: (0, 0))
    return pl.pallas_call(
        _qk_rope_kernel,
        grid=(m // tm,),
        in_specs=[row_spec(3), row_spec(0), vec_spec, vec_spec,
                  pl.BlockSpec((LANES, LANES), lambda i: (0, 0)), tab_spec, tab_spec, tab_spec],
        out_specs=[row_spec(0), row_spec(0), row_spec(0)],
        out_shape=[jax.ShapeDtypeStruct((m, QK_W), BF16), jax.ShapeDtypeStruct((m, QK_W), F32),
                   jax.ShapeDtypeStruct((m, QK_W), BF16)],
        compiler_params=_cparams("parallel"),
        name="qk_rope",
    )(z_a, z_b, qw, kw, ones_bd, *tables)


def _lambda_value(lq1_ref, lk1_ref, lq2_ref, lk2_ref):
    a = jnp.sum(lq1_ref[...] * lk1_ref[...], axis=-1, keepdims=True)
    b = jnp.sum(lq2_ref[...] * lk2_ref[...], axis=-1, keepdims=True)
    return jnp.exp(a) - jnp.exp(b) + LAM_INIT


def _sub_layer_norm(o, sw):
    ms = jnp.mean(o * o, axis=-1, keepdims=True)
    return o * lax.rsqrt(ms + EPS) * sw * (1.0 - LAM_INIT)


def _diff_attn_kernel(lq1_ref, lk1_ref, lq2_ref, lk2_ref, sw_ref, q_ref, k_ref, v_ref, o_ref, *, seq, tq):
    lam = _lambda_value(lq1_ref, lk1_ref, lq2_ref, lk2_ref)
    sw = sw_ref[...]
    for i in range(seq // tq):
        n_k = (i + 1) * tq
        q = q_ref[i * tq:(i + 1) * tq, :]
        k = k_ref[0:n_k, :]
        v = v_ref[0:n_k, :].astype(BF16)
        visible = (lax.broadcasted_iota(jnp.int32, (tq, tq), 1) <= lax.broadcasted_iota(jnp.int32, (tq, tq), 0))

        def softmax_parts(qc, kc):
            s = lax.dot_general(qc, kc, (((1,), (1,)), ((), ())), preferred_element_type=F32)
            s_diag = jnp.where(visible, s[:, n_k - tq:], NEG_INF)
            m = jnp.max(s_diag, axis=-1, keepdims=True)
            if n_k > tq:
                s_past = s[:, :n_k - tq]
                m = jnp.maximum(m, jnp.max(s_past, axis=-1, keepdims=True))
                e = jnp.concatenate([jnp.exp(s_past - m), jnp.exp(s_diag - m)], axis=-1)
            else:
                e = jnp.exp(s_diag - m)
            return e, jnp.sum(e, axis=-1, keepdims=True)

        e1, l1 = softmax_parts(q[:, :DK], k[:, :DK])
        e2, l2 = softmax_parts(q[:, DK:], k[:, DK:])
        a = e1 * (1.0 / l1) - e2 * (lam / l2)
        o = jnp.dot(a.astype(BF16), v, preferred_element_type=F32)
        o_ref[i * tq:(i + 1) * tq, :] = _sub_layer_norm(o, sw).astype(o_ref.dtype)


def diff_attn_prompt(q, k, z_b, lam_vecs, subln_w, batch, seq, tq):
    vec = pl.BlockSpec((1, DK), lambda b, h: (0, 0))
    head = lambda off: pl.BlockSpec((seq, HEAD_W), lambda b, h: (b, h + off))
    return pl.pallas_call(
        functools.partial(_diff_attn_kernel, seq=seq, tq=tq),
        grid=(batch, N_HEADS),
        in_specs=[vec, vec, vec, vec, pl.BlockSpec((1, DV), lambda b, h: (0, 0)),
                  head(0), head(0), head(N_HEADS)],
        out_specs=head(0),
        out_shape=jax.ShapeDtypeStruct((batch * seq, QK_W), BF16),
        compiler_params=_cparams("parallel", "parallel"),
        name="diff_attn_prompt",
    )(*lam_vecs, subln_w.reshape(1, DV), q, k, z_b)


def _diff_attn_decode_kernel(pt_ref, lq1_ref, lk1_ref, lq2_ref, lk2_ref, sw_ref, q_ref, kn_ref, vn_ref, *rest,
                             pages_per_step):
    del pt_ref
    k_refs = rest[:pages_per_step]
    v_refs = rest[pages_per_step:2 * pages_per_step]
    o_ref, qs_ref, m_ref, l_ref, acc_ref = rest[2 * pages_per_step:]
    j = pl.program_id(1)
    page_rows = PAGE_SIZE * N_HEADS

    @pl.when(j == 0)
    def _():
        q = q_ref[...]
        lane = lax.broadcasted_iota(jnp.int32, q.shape, 1)
        qrows = jnp.concatenate([jnp.where(lane < DK, q, 0.0), jnp.where(lane >= DK, q, 0.0)], axis=0)
        qs_ref[...] = qrows.astype(BF16)
        k_new = kn_ref[...].astype(BF16).astype(F32)
        s_new = jnp.sum(qrows * jnp.concatenate([k_new, k_new], axis=0), axis=-1, keepdims=True)
        m_ref[...] = jnp.broadcast_to(s_new, m_ref.shape)
        l_ref[...] = jnp.ones(l_ref.shape, F32)
        acc_ref[...] = jnp.concatenate([vn_ref[...], vn_ref[...]], axis=0)

    qs = qs_ref[...]
    own = (lax.broadcasted_iota(jnp.int32, (2 * N_HEADS, page_rows), 1) % N_HEADS
           == lax.broadcasted_iota(jnp.int32, (2 * N_HEADS, page_rows), 0) % N_HEADS)
    s = jnp.concatenate(
        [jnp.where(own, lax.dot_general(qs, k_refs[r][...].astype(BF16), (((1,), (1,)), ((), ())),
                                        preferred_element_type=F32), NEG_INF)
         for r in range(pages_per_step)], axis=-1)
    m_old = m_ref[:, 0:1]
    m_new = jnp.maximum(m_old, jnp.max(s, axis=-1, keepdims=True))
    alpha = jnp.exp(m_old - m_new)
    p = jnp.exp(s - m_new)
    l_ref[...] = jnp.broadcast_to(alpha * l_ref[:, 0:1] + jnp.sum(p, axis=-1, keepdims=True), l_ref.shape)
    m_ref[...] = jnp.broadcast_to(m_new, m_ref.shape)
    pv = acc_ref[...] * alpha
    for r in range(pages_per_step):
        pv = pv + jnp.dot(p[:, r * page_rows:(r + 1) * page_rows].astype(BF16), v_refs[r][...].astype(BF16),
                          preferred_element_type=F32)
    acc_ref[...] = pv

    @pl.when(j == pl.num_programs(1) - 1)
    def _():
        lam = _lambda_value(lq1_ref, lk1_ref, lq2_ref, lk2_ref)
        o = acc_ref[...] / l_ref[:, 0:1]
        od = o[0:N_HEADS, :] - lam * o[N_HEADS:2 * N_HEADS, :]
        o_ref[...] = _sub_layer_norm(od, sw_ref[...])


def diff_attn_decode(q, k_new, v_new, cache_k, cache_v, page_table, lam_vecs, subln_w, pages_per_step):
    n_batch, n_pages = page_table.shape
    vec = pl.BlockSpec((1, DK), lambda b, j, pt: (0, 0))
    tok = lambda: pl.BlockSpec((None, N_HEADS, HEAD_W), lambda b, j, pt: (b, 0, 0))

    def page_spec(r):
        return pl.BlockSpec((None, PAGE_SIZE * N_HEADS, HEAD_W),
                            lambda b, j, pt: (pt[b, j * pages_per_step + r], 0, 0))

    grid_spec = pltpu.PrefetchScalarGridSpec(
        num_scalar_prefetch=1,
        grid=(n_batch, n_pages // pages_per_step),
        in_specs=[vec, vec, vec, vec, pl.BlockSpec((1, DV), lambda b, j, pt: (0, 0)), tok(), tok(), tok()]
        + [page_spec(r) for r in range(pages_per_step)] + [page_spec(r) for r in range(pages_per_step)],
        out_specs=tok(),
        scratch_shapes=[pltpu.VMEM((2 * N_HEADS, HEAD_W), BF16), pltpu.VMEM((2 * N_HEADS, LANES), F32),
                        pltpu.VMEM((2 * N_HEADS, LANES), F32), pltpu.VMEM((2 * N_HEADS, HEAD_W), F32)],
    )
    return pl.pallas_call(
        functools.partial(_diff_attn_decode_kernel, pages_per_step=pages_per_step),
        grid_spec=grid_spec,
        out_shape=jax.ShapeDtypeStruct((n_batch, N_HEADS, HEAD_W), F32),
        compiler_params=_cparams("parallel", "arbitrary"),
        name="diff_attn_decode",
    )(page_table, *lam_vecs, subln_w.reshape(1, DV), q, k_new, v_new,
      *([cache_k] * pages_per_step), *([cache_v] * pages_per_step))


def _mem_attn_kernel(q_ref, w_ref, mk_ref, mv_ref, o_ref):
    q = q_ref[...].astype(F32)
    ms = jnp.mean(q * q, axis=-1, keepdims=True)
    qn = (q * lax.rsqrt(ms + EPS) * w_ref[...] * (MEM_HD ** -0.5)).astype(BF16)
    s = lax.dot_general(qn, mk_ref[...].astype(BF16), (((1,), (1,)), ((), ())), preferred_element_type=F32)
    e = jnp.exp(s - jnp.max(s, axis=-1, keepdims=True))
    p = e * (1.0 / jnp.sum(e, axis=-1, keepdims=True))
    o_ref[...] = jnp.dot(p.astype(BF16), mv_ref[...].astype(BF16), preferred_element_type=F32).astype(o_ref.dtype)


def mem_attn(z_q, q_col, mq_norm_w, mk, mv, batch, seq, tq):
    nq = seq // tq
    return pl.pallas_call(
        _mem_attn_kernel,
        grid=(batch, MEM_HEADS, nq),
        in_specs=[pl.BlockSpec((tq, MEM_HD), lambda b, h, i: (b * nq + i, q_col + h)),
                  pl.BlockSpec((1, MEM_HD), lambda b, h, i: (0, 0)),
                  pl.BlockSpec((N_MEM, MEM_HD), lambda b, h, i: (b, h)),
                  pl.BlockSpec((N_MEM, MEM_HD), lambda b, h, i: (b, h))],
        out_specs=pl.BlockSpec((tq, MEM_HD), lambda b, h, i: (b * nq + i, h)),
        out_shape=jax.ShapeDtypeStruct((batch * seq, CROSS_W), BF16),
        compiler_params=_cparams("parallel", "parallel", "parallel"),
        name="mem_attn",
    )(z_q, mq_norm_w.reshape(1, MEM_HD), mk, mv)


def _merge_tail(zb, conv, od_ref, om_ref, g0_ref, g1_ref, g2_ref, wc_ref, wd_ref, wx_ref, o_ref):
    yc = jnp.dot((zb * conv).astype(BF16), wc_ref[...], preferred_element_type=F32)
    yd = jnp.dot(od_ref[...], wd_ref[...], preferred_element_type=F32)
    ym = jnp.dot(om_ref[...], wx_ref[...], preferred_element_type=F32)
    merged = (jax.nn.sigmoid(g0_ref[...].astype(F32)) * yc + jax.nn.sigmoid(g1_ref[...].astype(F32)) * yd
              + jax.nn.sigmoid(g2_ref[...].astype(F32)) * ym)
    o_ref[...] = merged.astype(o_ref.dtype)


def _merge_prompt_kernel(zb_ref, zc_ref, zx_ref, hc_ref, hx_ref, cw_ref, *rest, tiles_per_seq, tm):
    u = zc_ref[...].astype(F32) * zx_ref[...].astype(F32)
    first = pl.program_id(0) % tiles_per_seq == 0
    halo = jnp.where(first, 0.0, hc_ref[...].astype(F32) * hx_ref[...].astype(F32))
    h1 = halo[SAMPLE_ROWS - 1:SAMPLE_ROWS, :]
    h2 = halo[SAMPLE_ROWS - 2:SAMPLE_ROWS - 1, :]
    row = lax.broadcasted_iota(jnp.int32, u.shape, 0)
    u1 = jnp.where(row == 0, h1, pltpu.roll(u, 1, axis=0))
    u2 = jnp.where(row == 0, h2, jnp.where(row == 1, h1, pltpu.roll(u, 2, axis=0)))
    conv = cw_ref[0:1, :] * u2 + cw_ref[1:2, :] * u1 + cw_ref[2:3, :] * u
    _merge_tail(zb_ref[...].astype(F32), conv, *rest)


def _merge_sample_kernel(zb_ref, zc_ref, zx_ref, p2_ref, p1_ref, cw_ref, *rest):
    u = zc_ref[...].astype(F32) * zx_ref[...].astype(F32)
    conv = cw_ref[0:1, :] * p2_ref[...] + cw_ref[1:2, :] * p1_ref[...] + cw_ref[2:3, :] * u
    _merge_tail(zb_ref[...].astype(F32), conv, *rest)


def merge_branches(z_a, prev, conv_w, o_diff, o_mem, z_g, w_conv_out, w_diff_out, w_cross_out, tm, seq):
    m = z_a.shape[0]
    col = lambda c, w: pl.BlockSpec((tm, w), lambda i: (i, c))
    const = lambda shape: pl.BlockSpec(shape, lambda i: (0, 0))
    tail_specs = [col(0, D_CONV), col(0, CROSS_W), col(0, D_MODEL), col(1, D_MODEL), col(2, D_MODEL),
                  const((D_CONV, D_MODEL)), const((QK_W, D_MODEL)), const((CROSS_W, D_MODEL))]
    tail_args = (o_diff, o_mem, z_g, z_g, z_g, w_conv_out, w_diff_out, w_cross_out)
    if prev is None:
        blocks = tm // SAMPLE_ROWS
        halo = lambda c: pl.BlockSpec((SAMPLE_ROWS, D_CONV), lambda i: (jnp.maximum(i * blocks - 1, 0), c))
        kern = functools.partial(_merge_prompt_kernel, tiles_per_seq=seq // tm, tm=tm)
        head_specs = [col(0, D_CONV), col(1, D_CONV), col(2, D_CONV), halo(1), halo(2), const((3, D_CONV))]
        head_args = (z_a, z_a, z_a, z_a, z_a, conv_w)
    else:
        kern = _merge_sample_kernel
        head_specs = [col(0, D_CONV), col(1, D_CONV), col(2, D_CONV), col(0, D_CONV), col(0, D_CONV),
                      const((3, D_CONV))]
        head_args = (z_a, z_a, z_a, prev[0], prev[1], conv_w)
    return pl.pallas_call(
        kern,
        grid=(m // tm,),
        in_specs=head_specs + tail_specs,
        out_specs=col(0, D_MODEL),
        out_shape=jax.ShapeDtypeStruct((m, D_MODEL), BF16),
        compiler_params=_cparams("parallel"),
        name="merge_branches",
    )(*head_args, *tail_args)


def _conv_state_kernel(zc_ref, zx_ref, o_ref):
    u = zc_ref[...].astype(F32) * zx_ref[...].astype(F32)
    o_ref[...] = u[SAMPLE_ROWS - 2:SAMPLE_ROWS, :]


def conv_state_prompt(z_a, batch, seq):
    blocks = seq // SAMPLE_ROWS
    tail = lambda c: pl.BlockSpec((SAMPLE_ROWS, D_CONV), lambda b: (b * blocks + blocks - 1, c))
    return pl.pallas_call(
        _conv_state_kernel,
        grid=(batch,),
        in_specs=[tail(1), tail(2)],
        out_specs=pl.BlockSpec((None, 2, D_CONV), lambda b: (b, 0, 0)),
        out_shape=jax.ShapeDtypeStruct((batch, 2, D_CONV), F32),
        compiler_params=_cparams("parallel"),
        name="conv_state_prompt",
    )(z_a, z_a)


def _sample_u_kernel(zc_ref, zx_ref, o_ref):
    o_ref[...] = zc_ref[...].astype(F32) * zx_ref[...].astype(F32)


def sample_u(z_a):
    m = z_a.shape[0]
    blk = lambda c: pl.BlockSpec((m, D_CONV), lambda i: (0, c))
    return pl.pallas_call(
        _sample_u_kernel,
        grid=(1,),
        in_specs=[blk(1), blk(2)],
        out_specs=blk(0),
        out_shape=jax.ShapeDtypeStruct((m, D_CONV), F32),
        compiler_params=_cparams("arbitrary"),
        name="sample_u",
    )(z_a, z_a)


def _out_proj_kernel(x_ref, a_ref, w_ref, nw_ref, x1_ref, h_ref):
    x1 = x_ref[...] + jnp.dot(a_ref[...], w_ref[...], preferred_element_type=F32)
    x1_ref[...] = x1
    ms = jnp.mean(x1 * x1, axis=-1, keepdims=True)
    h_ref[...] = (x1 * lax.rsqrt(ms + EPS) * nw_ref[...]).astype(h_ref.dtype)


def out_proj(x, merged, w_o, norm2_w, tm):
    m, d = x.shape
    tm = min(tm, m)
    row = pl.BlockSpec((tm, d), lambda i: (i, 0))
    return pl.pallas_call(
        _out_proj_kernel,
        grid=(m // tm,),
        in_specs=[row, row, pl.BlockSpec((d, d), lambda i: (0, 0)), pl.BlockSpec((1, d), lambda i: (0, 0))],
        out_specs=[row, row],
        out_shape=[jax.ShapeDtypeStruct((m, d), F32), jax.ShapeDtypeStruct((m, d), BF16)],
        compiler_params=_cparams("parallel"),
        name="out_proj",
    )(x, merged, w_o, norm2_w.reshape(1, d))


def _top_values(x, count):
    vals = []
    for _ in range(count):
        m = jnp.max(x, axis=0, keepdims=True)
        vals.append(m)
        x = jnp.where(x == m, NEG_INF, x)
    return vals


def _peer_tables(qp_ref, sk_ref, s1_ref, e1_ref, s2_ref, e2_ref, t_ref):
    for h in range(PEER_HEADS):
        s = []
        for c in range(2):
            qc = qp_ref[:, (2 * h + c) * N_KEYS:(2 * h + c + 1) * N_KEYS]
            s.append(lax.dot_general(sk_ref[c], qc, (((1,), (1,)), ((), ())), preferred_element_type=F32))
        v1 = _top_values(s[0], PEER_TOPK)
        v2 = _top_values(s[1], PEER_TOPK)
        cands = []
        for a in range(PEER_TOPK):
            for b in range(PEER_TOPK // (a + 1)):
                cands.append(v1[a] + v2[b])
        cand = jnp.concatenate(cands, axis=0)
        top = _top_values(cand, PEER_TOPK)
        smax = v1[0] + v2[0]
        z = jnp.zeros_like(smax)
        for r in range(PEER_TOPK):
            z = z + jnp.exp(top[r] - smax)
        rows = slice(h * N_KEYS, (h + 1) * N_KEYS)
        s1_ref[rows, :] = s[0]
        s2_ref[rows, :] = s[1]
        e1_ref[rows, :] = jnp.exp(s[0] - v1[0]) / z
        e2_ref[rows, :] = jnp.exp(s[1] - v2[0])
        t_ref[h:h + 1, :] = top[PEER_TOPK - 1]


def _peer_chunk(c, n_chunks, ec, h_ref, u_ref, v_ref, acc_ref, w_prev_ref, w_next_ref,
                s1_ref, e1_ref, s2_ref, e2_ref, t_ref):
    chunks = ec // N_KEYS
    first = jnp.minimum(c, n_chunks - 1) * chunks
    tb = h_ref.shape[0]
    tw = min(tb, 2 * LANES)
    for tj in range(tb // tw):
        cols = slice(tj * tw, (tj + 1) * tw)
        acc_ref[:, cols] += lax.dot_general(v_ref[...], w_prev_ref[:, cols], (((0,), (0,)), ((), ())),
                                            preferred_element_type=F32)
        a = lax.dot_general(u_ref[...], h_ref[cols, :], (((1,), (1,)), ((), ())), preferred_element_type=F32)
        act = 0.5 * a * (1.0 + lax.erf(a * INV_SQRT2))
        for il in range(chunks):
            gate = None
            for h in range(PEER_HEADS):
                row = h * N_KEYS + first + il
                s1 = s1_ref[pl.ds(row, 1), cols]
                e1 = e1_ref[pl.ds(row, 1), cols]
                keys = slice(h * N_KEYS, (h + 1) * N_KEYS)
                pair = s1 + s2_ref[keys, cols]
                g = jnp.where(pair >= t_ref[h:h + 1, cols], e1 * e2_ref[keys, cols], 0.0)
                gate = g if gate is None else gate + g
            rows = slice(il * N_KEYS, (il + 1) * N_KEYS)
            w_next_ref[rows, cols] = (act[rows, :] * gate).astype(BF16)


def _peer_kernel(h_ref, qp_ref, sk_ref, x1_ref, u_ref, v_ref, o_ref,
                 acc_ref, wa_ref, wb_ref, s1_ref, e1_ref, s2_ref, e2_ref, t_ref, *, ec, n_chunks):
    c = pl.program_id(1)

    @pl.when(c == 0)
    def _():
        acc_ref[...] = jnp.zeros(acc_ref.shape, F32)
        wa_ref[...] = jnp.zeros(wa_ref.shape, BF16)
        _peer_tables(qp_ref, sk_ref, s1_ref, e1_ref, s2_ref, e2_ref, t_ref)

    step = functools.partial(_peer_chunk, c, n_chunks, ec, h_ref, u_ref, v_ref, acc_ref)
    tables = (s1_ref, e1_ref, s2_ref, e2_ref, t_ref)

    @pl.when(c % 2 == 0)
    def _():
        step(wa_ref, wb_ref, *tables)

    @pl.when(c % 2 == 1)
    def _():
        step(wb_ref, wa_ref, *tables)

    @pl.when(c == n_chunks)
    def _():
        o_ref[...] = x1_ref[...] + acc_ref[...].T


def peer_layer(x1, h2, qp, sub_keys, peer_u, peer_v, tb, ec):
    m, d = x1.shape
    tb = min(tb, m)
    n_chunks = N_EXPERTS // ec
    tok = lambda: pl.BlockSpec((tb, d), lambda n, c: (n, 0))
    tok_once = lambda: pl.BlockSpec((tb, d), lambda n, c: (n, 0), pipeline_mode=pl.Buffered(1))
    tab = lambda: pltpu.VMEM((PEER_HEADS * N_KEYS, tb), F32)
    wbuf = lambda: pltpu.VMEM((ec, tb), BF16)
    return pl.pallas_call(
        functools.partial(_peer_kernel, ec=ec, n_chunks=n_chunks),
        grid=(m // tb, n_chunks + 1),
        in_specs=[tok(), tok_once(), pl.BlockSpec((2, N_KEYS, D_KEY // 2), lambda n, c: (0, 0, 0)), tok_once(),
                  pl.BlockSpec((ec, d), lambda n, c: (jnp.minimum(c, n_chunks - 1), 0)),
                  pl.BlockSpec((ec, d), lambda n, c: (jnp.maximum(c - 1, 0), 0))],
        out_specs=tok(),
        out_shape=jax.ShapeDtypeStruct((m, d), F32),
        scratch_shapes=[pltpu.VMEM((d, tb), F32), wbuf(), wbuf(), tab(), tab(), tab(), tab(),
                        pltpu.VMEM((PEER_HEADS, tb), F32)],
        compiler_params=_cparams("parallel", "arbitrary"),
        name="peer_layer",
    )(h2, qp, sub_keys, x1, peer_u, peer_v)


def _token_stack(x, rope_tabs, table_blocks, w, tiles):
    h = rmsnorm_cast(x, w["norm1_w"], tiles["norm"])
    w_in = w["w_in"]
    z_a = matmul(h, w_in, BF16, tiles["mm"], 1024, 0, KV_AT)
    z_b = matmul(h, w_in, F32, tiles["mm"], 1024, KV_AT, MEMQ_AT - KV_AT)
    z_m = matmul(h, w_in, BF16, tiles["mm"], 1024, MEMQ_AT, GATES_AT - MEMQ_AT)
    z_g = matmul(h, w_in, BF16, tiles["mm"], 1024, GATES_AT, 3 * D_MODEL)
    q, k, k_bf = qk_rope(z_a, z_b, w["q_norm_w"], w["k_norm_w"], rope_tabs, tiles["rope"], table_blocks)
    return z_a, z_b, z_m, z_g, q, k, k_bf


def _finish(x, merged, w, tiles):
    x1, h2 = out_proj(x, merged, w["w_o"], w["norm2_w"], tiles["out"])
    qp = matmul(h2, w["w_peer_q"], BF16, tiles["mm"], 1024)
    return peer_layer(x1, h2, qp, w["sub_keys"], w["peer_u"], w["peer_v"], tiles["peer_tb"], tiles["peer_ec"])


def kernel(x_prompt, x_sample, mem_prompt, cache_k, cache_v, page_table, state_conv, cache_mem_k, cache_mem_v,
           norm1_w, w_in, conv_w, q_norm_w, k_norm_w, lambda_q1, lambda_k1, lambda_q2, lambda_k2, subln_w,
           mem_norm_w, w_mem_kv, mq_norm_w, mk_norm_w, w_conv_out, w_diff_out, w_cross_out, w_o, norm2_w,
           w_peer_q, sub_keys, peer_u, peer_v):
    batch, seq, d = x_prompt.shape
    n_dec = x_sample.shape[0]
    l = 0
    w = {
        "norm1_w": norm1_w[l], "norm2_w": norm2_w[l], "q_norm_w": q_norm_w[l], "k_norm_w": k_norm_w[l],
        "w_in": w_in[l].astype(BF16),
        "w_o": w_o[l].astype(BF16), "w_peer_q": w_peer_q[l].astype(BF16), "sub_keys": sub_keys[l].astype(BF16),
        "peer_u": peer_u[l].astype(BF16), "peer_v": peer_v[l].astype(BF16),
    }
    w_conv_out_b = w_conv_out[l].astype(BF16)
    w_diff_out_b = w_diff_out[l].astype(BF16)
    w_cross_out_b = w_cross_out[l].astype(BF16)
    lam_vecs = tuple(v[l].reshape(1, DK) for v in (lambda_q1, lambda_k1, lambda_q2, lambda_k2))

    tiles_p = {"norm": 512, "mm": 1024, "rope": 512, "out": 512, "peer_tb": 512, "peer_ec": 1024}
    xp = x_prompt.reshape(batch * seq, d)
    tabs_p = _rope_tables(jnp.arange(seq, dtype=jnp.int32))
    z_a, z_b, z_m, z_g, q, k, k_bf = _token_stack(xp, tabs_p, seq // tiles_p["rope"], w, tiles_p)
    h_mem = rmsnorm_cast(mem_prompt.reshape(batch * N_MEM, d), mem_norm_w[l], 512)
    kv_mem = matmul(h_mem, w_mem_kv[l].astype(BF16), F32, 1024, 1024)
    mk_p = group_norm(kv_mem, mk_norm_w[l], 512)
    mv_p = kv_mem[:, CROSS_W:]
    o_diff = diff_attn_prompt(q, k_bf, z_b, lam_vecs, subln_w[l], batch, seq, 256)
    o_mem = mem_attn(z_m, 0, mq_norm_w[l], mk_p, mv_p, batch, seq, 1024)
    merged = merge_branches(z_a, None, conv_w[l], o_diff, o_mem, z_g, w_conv_out_b, w_diff_out_b, w_cross_out_b,
                            256, seq)
    conv_p = conv_state_prompt(z_a, batch, seq)
    y_prompt = _finish(xp, merged, w, tiles_p).reshape(batch, seq, d)

    tiles_s = {"norm": 128, "mm": 128, "rope": 128, "out": 128, "peer_tb": 128, "peer_ec": 1024}
    m_s = n_dec * SAMPLE_ROWS
    xs = jnp.pad(x_sample, ((0, 0), (0, SAMPLE_ROWS - x_sample.shape[1]), (0, 0))).reshape(m_s, d)
    tabs_s = _rope_tables(jnp.full((m_s,), PAST_LEN, jnp.int32))
    zs_a, zs_b, zs_m, zs_g, qs, ks, _ = _token_stack(xs, tabs_s, 1, w, tiles_s)
    pages = cache_k.shape[1]
    new_tok = lambda a: a.reshape(n_dec, SAMPLE_ROWS, N_HEADS, HEAD_W)[:, 0].astype(F32)
    os_heads = diff_attn_decode(new_tok(qs), new_tok(ks), new_tok(zs_b[:, QK_W:]),
                                cache_k[l].reshape(pages, PAGE_SIZE * N_HEADS, HEAD_W),
                                cache_v[l].reshape(pages, PAGE_SIZE * N_HEADS, HEAD_W),
                                page_table, lam_vecs, subln_w[l], 8)
    os_diff = jnp.pad(os_heads.reshape(n_dec, 1, QK_W).astype(BF16),
                      ((0, 0), (0, SAMPLE_ROWS - 1), (0, 0))).reshape(m_s, QK_W)
    os_mem = mem_attn(zs_m, 0, mq_norm_w[l], cache_mem_k[l].reshape(n_dec * N_MEM, CROSS_W),
                      cache_mem_v[l].reshape(n_dec * N_MEM, CROSS_W), n_dec, SAMPLE_ROWS, SAMPLE_ROWS)
    pad_state = lambda r: jnp.pad(state_conv[l][:, r:r + 1], ((0, 0), (0, SAMPLE_ROWS - 1), (0, 0))).reshape(m_s, D_CONV)
    merged_s = merge_branches(zs_a, (pad_state(0), pad_state(1)), conv_w[l], os_diff, os_mem, zs_g,
                              w_conv_out_b, w_diff_out_b, w_cross_out_b, 128, SAMPLE_ROWS)
    us = sample_u(zs_a).reshape(n_dec, SAMPLE_ROWS, D_CONV)[:, 0:1]
    conv_s = jnp.concatenate([state_conv[l][:, 1:2], us], axis=1)
    y_sample = _finish(xs, merged_s, w, tiles_s).reshape(n_dec, SAMPLE_ROWS, d)[:, 0:1]

    first = lambda a, n: a.reshape(n, SAMPLE_ROWS, N_HEADS, HEAD_W)[:, 0:1]
    return (y_prompt, y_sample,
            k.reshape(1, batch, seq, N_HEADS, HEAD_W), z_b[:, QK_W:].reshape(1, batch, seq, N_HEADS, DV),
            conv_p[None], mk_p.reshape(1, batch, N_MEM, MEM_HEADS, MEM_HD),
            mv_p.reshape(1, batch, N_MEM, MEM_HEADS, MEM_HD),
            first(ks, n_dec)[None], first(zs_b[:, QK_W:], n_dec)[None], conv_s[None])
```

```python
import functools
import math

import jax
import jax.numpy as jnp
import numpy as np
from jax import lax
from jax.experimental import pallas as pl
from jax.experimental.pallas import tpu as pltpu

F32 = jnp.float32
BF16 = jnp.bfloat16

D_MODEL = 2048
D_CONV = D_MODEL // 2
N_HEADS = 8
DK = D_MODEL // 32
DV = 2 * DK
HEAD_W = 2 * DK
QK_W = N_HEADS * 2 * DK
ROT_DIM = DK // 4
ROPE_THETA = 500000.0
N_MEM = 256
MEM_HEADS = 4
MEM_HD = D_MODEL // 8
CROSS_W = MEM_HEADS * MEM_HD
N_KEYS = 128
N_EXPERTS = N_KEYS * N_KEYS
PEER_HEADS = 8
PEER_TOPK = 16
D_KEY = 256
PAST_LEN = 16384
PAGE_SIZE = 128
EPS = 1e-6
NEG_INF = -1e30
UNREACHABLE = 3e38
KV_AT = 3 * D_CONV + QK_W
MEMQ_AT = KV_AT + 2 * QK_W
GATES_AT = MEMQ_AT + CROSS_W
LAM_INIT = 0.8 - 0.6 * math.exp(-0.3 * 0)
INV_SQRT2 = 1.0 / math.sqrt(2.0)

LANES = 128
SAMPLE_ROWS = 16
VMEM_LIMIT = 56 * 1024 * 1024


def _cparams(*sem, flags=None):
    return pltpu.CompilerParams(dimension_semantics=sem, vmem_limit_bytes=VMEM_LIMIT, flags=flags)


def _rmsnorm_kernel(x_ref, w_ref, o_ref):
    x = x_ref[...]
    ms = jnp.mean(x * x, axis=-1, keepdims=True)
    o_ref[...] = (x * lax.rsqrt(ms + EPS) * w_ref[...]).astype(o_ref.dtype)


def rmsnorm_cast(x, w, tm):
    m, d = x.shape
    return pl.pallas_call(
        _rmsnorm_kernel,
        grid=(m // tm,),
        in_specs=[pl.BlockSpec((tm, d), lambda i: (i, 0)), pl.BlockSpec((1, d), lambda i: (0, 0))],
        out_specs=pl.BlockSpec((tm, d), lambda i: (i, 0)),
        out_shape=jax.ShapeDtypeStruct((m, d), BF16),
        compiler_params=_cparams("parallel"),
        name="rmsnorm_cast",
    )(x, w.reshape(1, d))


def _matmul_kernel(a_ref, b_ref, o_ref):
    o_ref[...] = jnp.dot(a_ref[...], b_ref[...], preferred_element_type=F32).astype(o_ref.dtype)


def matmul(a, b, out_dtype, tm, tn, col0=0, n=None):
    m, k = a.shape
    n = b.shape[1] if n is None else n
    tm = min(tm, m)
    tn = min(tn, n)
    first = col0 // tn
    return pl.pallas_call(
        _matmul_kernel,
        grid=(m // tm, n // tn),
        in_specs=[pl.BlockSpec((tm, k), lambda i, j: (i, 0)), pl.BlockSpec((k, tn), lambda i, j: (0, j + first))],
        out_specs=pl.BlockSpec((tm, tn), lambda i, j: (i, j)),
        out_shape=jax.ShapeDtypeStruct((m, n), out_dtype),
        compiler_params=_cparams("parallel", "arbitrary"),
        name="matmul",
    )(a, b)


def _group_norm_kernel(x_ref, w_ref, o_ref):
    for g in range(MEM_HEADS):
        c = x_ref[:, g * MEM_HD:(g + 1) * MEM_HD]
        ms = jnp.mean(c * c, axis=-1, keepdims=True)
        o_ref[:, g * MEM_HD:(g + 1) * MEM_HD] = c * lax.rsqrt(ms + EPS) * w_ref[...]


def group_norm(x, w, tm):
    m = x.shape[0]
    return pl.pallas_call(
        _group_norm_kernel,
        grid=(m // tm,),
        in_specs=[pl.BlockSpec((tm, CROSS_W), lambda i: (i, 0)), pl.BlockSpec((1, MEM_HD), lambda i: (0, 0))],
        out_specs=pl.BlockSpec((tm, CROSS_W), lambda i: (i, 0)),
        out_shape=jax.ShapeDtypeStruct((m, CROSS_W), F32),
        compiler_params=_cparams("parallel"),
        name="group_norm",
    )(x, w.reshape(1, MEM_HD))


def _sub_head_norm_rope(x, w, ones_bd, ca, cm, cp):
    x2 = x * x
    hi = x2.astype(BF16)
    lo = (x2 - hi.astype(F32)).astype(BF16)
    ss = jnp.dot(hi, ones_bd, preferred_element_type=F32) + jnp.dot(lo, ones_bd, preferred_element_type=F32)
    y = x * lax.rsqrt(ss * (1.0 / DK) + EPS) * w
    half = ROT_DIM // 2
    y_up = pltpu.roll(y, LANES - half, axis=1)
    y_dn = pltpu.roll(y, half, axis=1)
    return y * ca + y_up * cm + y_dn * cp


def _qk_rope_kernel(zq_ref, zk_ref, qw_ref, kw_ref, bd_ref, ca_ref, cm_ref, cp_ref, q_ref, k_ref, kb_ref):
    ones_bd = bd_ref[...]
    ca, cm, cp = ca_ref[...], cm_ref[...], cp_ref[...]
    for c in range(QK_W // LANES):
        sl = slice(c * LANES, (c + 1) * LANES)
        q = _sub_head_norm_rope(zq_ref[:, sl].astype(F32), qw_ref[...], ones_bd, ca, cm, cp)
        q_ref[:, sl] = (q * (DK ** -0.5)).astype(BF16)
        k = _sub_head_norm_rope(zk_ref[:, sl], kw_ref[...], ones_bd, ca, cm, cp)
        k_ref[:, sl] = k
        kb_ref[:, sl] = k.astype(BF16)


def _rope_tables(pos):
    half = ROT_DIM // 2
    inv_freq = ROPE_THETA ** (-jnp.arange(half, dtype=F32) / half)
    ang = pos.astype(F32)[:, None] * inv_freq[None, :]
    cos, sin = jnp.cos(ang), jnp.sin(ang)
    t = pos.shape[0]
    ones = jnp.ones((t, DK - ROT_DIM), F32)
    zeros = jnp.zeros((t, DK - ROT_DIM), F32)
    zh = jnp.zeros((t, half), F32)
    ca = jnp.concatenate([cos, cos, ones], axis=-1)
    cm = jnp.concatenate([-sin, zh, zeros], axis=-1)
    cp = jnp.concatenate([zh, sin, zeros], axis=-1)
    return tuple(jnp.concatenate([a, a], axis=-1) for a in (ca, cm, cp))


def qk_rope(z_a, z_b, q_norm_w, k_norm_w, tables, tm, table_blocks):
    m = z_a.shape[0]
    qw = jnp.tile(q_norm_w.reshape(1, DK), (1, 2))
    kw = jnp.tile(k_norm_w.reshape(1, DK), (1, 2))
    grp = np.arange(LANES) // DK
    ones_bd = jnp.asarray(grp[:, None] == grp[None, :], BF16)
    row_spec = lambda col: pl.BlockSpec((tm, QK_W), lambda i: (i, col))
    tab_spec = pl.BlockSpec((tm, LANES), lambda i: (i % table_blocks, 0))
    vec_spec = pl.BlockSpec((1, LANES), lambda i: (0, 0))
    return pl.pallas_call(
        _qk_rope_kernel,
        grid=(m // tm,),
        in_specs=[row_spec(3), row_spec(0), vec_spec, vec_spec,
                  pl.BlockSpec((LANES, LANES), lambda i: (0, 0)), tab_spec, tab_spec, tab_spec],
        out_specs=[row_spec(0), row_spec(0), row_spec(0)],
        out_shape=[jax.ShapeDtypeStruct((m, QK_W), BF16), jax.ShapeDtypeStruct((m, QK_W), F32),
                   jax.ShapeDtypeStruct((m, QK_W), BF16)],
        compiler_params=_cparams("parallel"),
        name="qk_rope",
    )(z_a, z_b, qw, kw, ones_bd, *tables)


def _lambda_value(lq1_ref, lk1_ref, lq2_ref, lk2_ref):
    a = jnp.sum(lq1_ref[...] * lk1_ref[...], axis=-1, keepdims=True)
    b = jnp.sum(lq2_ref[...] * lk2_ref[...], axis=-1, keepdims=True)
    return jnp.exp(a) - jnp.exp(b) + LAM_INIT


def _sub_layer_norm(o, sw):
    ms = jnp.mean(o * o, axis=-1, keepdims=True)
    return o * lax.rsqrt(ms + EPS) * sw * (1.0 - LAM_INIT)


def _diff_attn_kernel(lq1_ref, lk1_ref, lq2_ref, lk2_ref, sw_ref, q_ref, k_ref, v_ref, o_ref, *, seq, tq):
    lam = _lambda_value(lq1_ref, lk1_ref, lq2_ref, lk2_ref)
    sw = sw_ref[...]
    for i in range(seq // tq):
        n_k = (i + 1) * tq
        q = q_ref[i * tq:(i + 1) * tq, :]
        k = k_ref[0:n_k, :]
        v = v_ref[0:n_k, :].astype(BF16)
        row = lax.broadcasted_iota(jnp.int32, (tq, n_k), 0) + i * tq
        col = lax.broadcasted_iota(jnp.int32, (tq, n_k), 1)
        visible = col <= row

        def softmax_parts(qc, kc):
            s = lax.dot_general(qc, kc, (((1,), (1,)), ((), ())), preferred_element_type=F32)
            s = jnp.where(visible, s, NEG_INF)
            e = jnp.exp(s - jnp.max(s, axis=-1, keepdims=True))
            return e, jnp.sum(e, axis=-1, keepdims=True)

        e1, l1 = softmax_parts(q[:, :DK], k[:, :DK])
        e2, l2 = softmax_parts(q[:, DK:], k[:, DK:])
        a = e1 * (1.0 / l1) - e2 * (lam / l2)
        o = jnp.dot(a.astype(BF16), v, preferred_element_type=F32)
        o_ref[i * tq:(i + 1) * tq, :] = _sub_layer_norm(o, sw).astype(o_ref.dtype)


def diff_attn_prompt(q, k, z_b, lam_vecs, subln_w, batch, seq, tq):
    vec = pl.BlockSpec((1, DK), lambda b, h: (0, 0))
    head = lambda off: pl.BlockSpec((seq, HEAD_W), lambda b, h: (b, h + off))
    return pl.pallas_call(
        functools.partial(_diff_attn_kernel, seq=seq, tq=tq),
        grid=(batch, N_HEADS),
        in_specs=[vec, vec, vec, vec, pl.BlockSpec((1, DV), lambda b, h: (0, 0)),
                  head(0), head(0), head(N_HEADS)],
        out_specs=head(0),
        out_shape=jax.ShapeDtypeStruct((batch * seq, QK_W), BF16),
        compiler_params=_cparams("parallel", "parallel"),
        name="diff_attn_prompt",
    )(*lam_vecs, subln_w.reshape(1, DV), q, k, z_b)


def _diff_attn_decode_kernel(pt_ref, lq1_ref, lk1_ref, lq2_ref, lk2_ref, sw_ref, q_ref, kn_ref, vn_ref, *rest,
                             pages_per_step):
    del pt_ref
    k_refs = rest[:pages_per_step]
    v_refs = rest[pages_per_step:2 * pages_per_step]
    o_ref, qs_ref, m_ref, l_ref, acc_ref = rest[2 * pages_per_step:]
    j = pl.program_id(1)
    page_rows = PAGE_SIZE * N_HEADS

    @pl.when(j == 0)
    def _():
        q = q_ref[...]
        lane = lax.broadcasted_iota(jnp.int32, q.shape, 1)
        qrows = jnp.concatenate([jnp.where(lane < DK, q, 0.0), jnp.where(lane >= DK, q, 0.0)], axis=0)
        qs_ref[...] = qrows.astype(BF16)
        k_new = kn_ref[...].astype(BF16).astype(F32)
        s_new = jnp.sum(qrows * jnp.concatenate([k_new, k_new], axis=0), axis=-1, keepdims=True)
        m_ref[...] = jnp.broadcast_to(s_new, m_ref.shape)
        l_ref[...] = jnp.ones(l_ref.shape, F32)
        acc_ref[...] = jnp.concatenate([vn_ref[...], vn_ref[...]], axis=0)

    qs = qs_ref[...]
    own = (lax.broadcasted_iota(jnp.int32, (2 * N_HEADS, page_rows), 1) % N_HEADS
           == lax.broadcasted_iota(jnp.int32, (2 * N_HEADS, page_rows), 0) % N_HEADS)
    s = jnp.concatenate(
        [jnp.where(own, lax.dot_general(qs, k_refs[r][...].astype(BF16), (((1,), (1,)), ((), ())),
                                        preferred_element_type=F32), NEG_INF)
         for r in range(pages_per_step)], axis=-1)
    m_old = m_ref[:, 0:1]
    m_new = jnp.maximum(m_old, jnp.max(s, axis=-1, keepdims=True))
    alpha = jnp.exp(m_old - m_new)
    p = jnp.exp(s - m_new)
    l_ref[...] = jnp.broadcast_to(alpha * l_ref[:, 0:1] + jnp.sum(p, axis=-1, keepdims=True), l_ref.shape)
    m_ref[...] = jnp.broadcast_to(m_new, m_ref.shape)
    pv = acc_ref[...] * alpha
    for r in range(pages_per_step):
        pv = pv + jnp.dot(p[:, r * page_rows:(r + 1) * page_rows].astype(BF16), v_refs[r][...].astype(BF16),
                          preferred_element_type=F32)
    acc_ref[...] = pv

    @pl.when(j == pl.num_programs(1) - 1)
    def _():
        lam = _lambda_value(lq1_ref, lk1_ref, lq2_ref, lk2_ref)
        o = acc_ref[...] / l_ref[:, 0:1]
        od = o[0:N_HEADS, :] - lam * o[N_HEADS:2 * N_HEADS, :]
        o_ref[...] = _sub_layer_norm(od, sw_ref[...])


def diff_attn_decode(q, k_new, v_new, cache_k, cache_v, page_table, lam_vecs, subln_w, pages_per_step):
    n_batch, n_pages = page_table.shape
    vec = pl.BlockSpec((1, DK), lambda b, j, pt: (0, 0))
    tok = lambda: pl.BlockSpec((None, N_HEADS, HEAD_W), lambda b, j, pt: (b, 0, 0))

    def page_spec(r):
        return pl.BlockSpec((None, PAGE_SIZE * N_HEADS, HEAD_W),
                            lambda b, j, pt: (pt[b, j * pages_per_step + r], 0, 0))

    grid_spec = pltpu.PrefetchScalarGridSpec(
        num_scalar_prefetch=1,
        grid=(n_batch, n_pages // pages_per_step),
        in_specs=[vec, vec, vec, vec, pl.BlockSpec((1, DV), lambda b, j, pt: (0, 0)), tok(), tok(), tok()]
        + [page_spec(r) for r in range(pages_per_step)] + [page_spec(r) for r in range(pages_per_step)],
        out_specs=tok(),
        scratch_shapes=[pltpu.VMEM((2 * N_HEADS, HEAD_W), BF16), pltpu.VMEM((2 * N_HEADS, LANES), F32),
                        pltpu.VMEM((2 * N_HEADS, LANES), F32), pltpu.VMEM((2 * N_HEADS, HEAD_W), F32)],
    )
    return pl.pallas_call(
        functools.partial(_diff_attn_decode_kernel, pages_per_step=pages_per_step),
        grid_spec=grid_spec,
        out_shape=jax.ShapeDtypeStruct((n_batch, N_HEADS, HEAD_W), F32),
        compiler_params=_cparams("parallel", "arbitrary"),
        name="diff_attn_decode",
    )(page_table, *lam_vecs, subln_w.reshape(1, DV), q, k_new, v_new,
      *([cache_k] * pages_per_step), *([cache_v] * pages_per_step))


def _mem_attn_kernel(q_ref, w_ref, mk_ref, mv_ref, o_ref):
    q = q_ref[...].astype(F32)
    ms = jnp.mean(q * q, axis=-1, keepdims=True)
    qn = (q * lax.rsqrt(ms + EPS) * w_ref[...] * (MEM_HD ** -0.5)).astype(BF16)
    s = lax.dot_general(qn, mk_ref[...].astype(BF16), (((1,), (1,)), ((), ())), preferred_element_type=F32)
    e = jnp.exp(s - jnp.max(s, axis=-1, keepdims=True))
    p = e * (1.0 / jnp.sum(e, axis=-1, keepdims=True))
    o_ref[...] = jnp.dot(p.astype(BF16), mv_ref[...].astype(BF16), preferred_element_type=F32).astype(o_ref.dtype)


def mem_attn(z_q, q_col, mq_norm_w, mk, mv, batch, seq, tq):
    nq = seq // tq
    return pl.pallas_call(
        _mem_attn_kernel,
        grid=(batch, MEM_HEADS, nq),
        in_specs=[pl.BlockSpec((tq, MEM_HD), lambda b, h, i: (b * nq + i, q_col + h)),
                  pl.BlockSpec((1, MEM_HD), lambda b, h, i: (0, 0)),
                  pl.BlockSpec((N_MEM, MEM_HD), lambda b, h, i: (b, h)),
                  pl.BlockSpec((N_MEM, MEM_HD), lambda b, h, i: (b, h))],
        out_specs=pl.BlockSpec((tq, MEM_HD), lambda b, h, i: (b * nq + i, h)),
        out_shape=jax.ShapeDtypeStruct((batch * seq, CROSS_W), BF16),
        compiler_params=_cparams("parallel", "parallel", "parallel"),
        name="mem_attn",
    )(z_q, mq_norm_w.reshape(1, MEM_HD), mk, mv)


def _merge_tail(zb, conv, od_ref, om_ref, g0_ref, g1_ref, g2_ref, wc_ref, wd_ref, wx_ref, o_ref):
    yc = jnp.dot((zb * conv).astype(BF16), wc_ref[...], preferred_element_type=F32)
    yd = jnp.dot(od_ref[...], wd_ref[...], preferred_element_type=F32)
    ym = jnp.dot(om_ref[...], wx_ref[...], preferred_element_type=F32)
    merged = (jax.nn.sigmoid(g0_ref[...].astype(F32)) * yc + jax.nn.sigmoid(g1_ref[...].astype(F32)) * yd
              + jax.nn.sigmoid(g2_ref[...].astype(F32)) * ym)
    o_ref[...] = merged.astype(o_ref.dtype)


def _merge_prompt_kernel(zb_ref, zc_ref, zx_ref, hc_ref, hx_ref, cw_ref, *rest, tiles_per_seq, tm):
    u = zc_ref[...].astype(F32) * zx_ref[...].astype(F32)
    first = pl.program_id(0) % tiles_per_seq == 0
    halo = jnp.where(first, 0.0, hc_ref[...].astype(F32) * hx_ref[...].astype(F32))
    h1 = halo[SAMPLE_ROWS - 1:SAMPLE_ROWS, :]
    h2 = halo[SAMPLE_ROWS - 2:SAMPLE_ROWS - 1, :]
    row = lax.broadcasted_iota(jnp.int32, u.shape, 0)
    u1 = jnp.where(row == 0, h1, pltpu.roll(u, 1, axis=0))
    u2 = jnp.where(row == 0, h2, jnp.where(row == 1, h1, pltpu.roll(u, 2, axis=0)))
    conv = cw_ref[0:1, :] * u2 + cw_ref[1:2, :] * u1 + cw_ref[2:3, :] * u
    _merge_tail(zb_ref[...].astype(F32), conv, *rest)


def _merge_sample_kernel(zb_ref, zc_ref, zx_ref, p2_ref, p1_ref, cw_ref, *rest):
    u = zc_ref[...].astype(F32) * zx_ref[...].astype(F32)
    conv = cw_ref[0:1, :] * p2_ref[...] + cw_ref[1:2, :] * p1_ref[...] + cw_ref[2:3, :] * u
    _merge_tail(zb_ref[...].astype(F32), conv, *rest)


def merge_branches(z_a, prev, conv_w, o_diff, o_mem, z_g, w_conv_out, w_diff_out, w_cross_out, tm, seq):
    m = z_a.shape[0]
    col = lambda c, w: pl.BlockSpec((tm, w), lambda i: (i, c))
    const = lambda shape: pl.BlockSpec(shape, lambda i: (0, 0))
    tail_specs = [col(0, D_CONV), col(0, CROSS_W), col(0, D_MODEL), col(1, D_MODEL), col(2, D_MODEL),
                  const((D_CONV, D_MODEL)), const((QK_W, D_MODEL)), const((CROSS_W, D_MODEL))]
    tail_args = (o_diff, o_mem, z_g, z_g, z_g, w_conv_out, w_diff_out, w_cross_out)
    if prev is None:
        blocks = tm // SAMPLE_ROWS
        halo = lambda c: pl.BlockSpec((SAMPLE_ROWS, D_CONV), lambda i: (jnp.maximum(i * blocks - 1, 0), c))
        kern = functools.partial(_merge_prompt_kernel, tiles_per_seq=seq // tm, tm=tm)
        head_specs = [col(0, D_CONV), col(1, D_CONV), col(2, D_CONV), halo(1), halo(2), const((3, D_CONV))]
        head_args = (z_a, z_a, z_a, z_a, z_a, conv_w)
    else:
        kern = _merge_sample_kernel
        head_specs = [col(0, D_CONV), col(1, D_CONV), col(2, D_CONV), col(0, D_CONV), col(0, D_CONV),
                      const((3, D_CONV))]
        head_args = (z_a, z_a, z_a, prev[0], prev[1], conv_w)
    return pl.pallas_call(
        kern,
        grid=(m // tm,),
        in_specs=head_specs + tail_specs,
        out_specs=col(0, D_MODEL),
        out_shape=jax.ShapeDtypeStruct((m, D_MODEL), BF16),
        compiler_params=_cparams("parallel"),
        name="merge_branches",
    )(*head_args, *tail_args)


def _conv_state_kernel(zc_ref, zx_ref, o_ref):
    u = zc_ref[...].astype(F32) * zx_ref[...].astype(F32)
    o_ref[...] = u[SAMPLE_ROWS - 2:SAMPLE_ROWS, :]


def conv_state_prompt(z_a, batch, seq):
    blocks = seq // SAMPLE_ROWS
    tail = lambda c: pl.BlockSpec((SAMPLE_ROWS, D_CONV), lambda b: (b * blocks + blocks - 1, c))
    return pl.pallas_call(
        _conv_state_kernel,
        grid=(batch,),
        in_specs=[tail(1), tail(2)],
        out_specs=pl.BlockSpec((None, 2, D_CONV), lambda b: (b, 0, 0)),
        out_shape=jax.ShapeDtypeStruct((batch, 2, D_CONV), F32),
        compiler_params=_cparams("parallel"),
        name="conv_state_prompt",
    )(z_a, z_a)


def _sample_u_kernel(zc_ref, zx_ref, o_ref):
    o_ref[...] = zc_ref[...].astype(F32) * zx_ref[...].astype(F32)


def sample_u(z_a):
    m = z_a.shape[0]
    blk = lambda c: pl.BlockSpec((m, D_CONV), lambda i: (0, c))
    return pl.pallas_call(
        _sample_u_kernel,
        grid=(1,),
        in_specs=[blk(1), blk(2)],
        out_specs=blk(0),
        out_shape=jax.ShapeDtypeStruct((m, D_CONV), F32),
        compiler_params=_cparams("arbitrary"),
        name="sample_u",
    )(z_a, z_a)


def _out_proj_kernel(x_ref, a_ref, w_ref, nw_ref, x1_ref, h_ref):
    x1 = x_ref[...] + jnp.dot(a_ref[...], w_ref[...], preferred_element_type=F32)
    x1_ref[...] = x1
    ms = jnp.mean(x1 * x1, axis=-1, keepdims=True)
    h_ref[...] = (x1 * lax.rsqrt(ms + EPS) * nw_ref[...]).astype(h_ref.dtype)


def out_proj(x, merged, w_o, norm2_w, tm):
    m, d = x.shape
    tm = min(tm, m)
    row = pl.BlockSpec((tm, d), lambda i: (i, 0))
    return pl.pallas_call(
        _out_proj_kernel,
        grid=(m // tm,),
        in_specs=[row, row, pl.BlockSpec((d, d), lambda i: (0, 0)), pl.BlockSpec((1, d), lambda i: (0, 0))],
        out_specs=[row, row],
        out_shape=[jax.ShapeDtypeStruct((m, d), F32), jax.ShapeDtypeStruct((m, d), BF16)],
        compiler_params=_cparams("parallel"),
        name="out_proj",
    )(x, merged, w_o, norm2_w.reshape(1, d))


def _top_values(x, count):
    vals = []
    for _ in range(count):
        m = jnp.max(x, axis=0, keepdims=True)
        vals.append(m)
        x = jnp.where(x == m, NEG_INF, x)
    return vals


def _peer_tables(qp_ref, sk_ref, thr_ref, e1_ref, s2_ref, e2_ref):
    for h in range(PEER_HEADS):
        s = []
        for c in range(2):
            qc = qp_ref[:, (2 * h + c) * N_KEYS:(2 * h + c + 1) * N_KEYS]
            s.append(lax.dot_general(sk_ref[c], qc, (((1,), (1,)), ((), ())), preferred_element_type=F32))
        v1 = _top_values(s[0], PEER_TOPK)
        v2 = _top_values(s[1], PEER_TOPK)
        cands = []
        for a in range(PEER_TOPK):
            for b in range(PEER_TOPK // (a + 1)):
                cands.append(v1[a] + v2[b])
        cand = jnp.concatenate(cands, axis=0)
        top = _top_values(cand, PEER_TOPK)
        smax = v1[0] + v2[0]
        z = jnp.zeros_like(smax)
        for r in range(PEER_TOPK):
            z = z + jnp.exp(top[r] - smax)
        t = top[PEER_TOPK - 1]
        thr = jnp.full(s[0].shape, UNREACHABLE, F32)
        for b in range(PEER_TOPK):
            thr = jnp.where(s[0] + v2[b] >= t, v2[b], thr)
        rows = slice(h * N_KEYS, (h + 1) * N_KEYS)
        thr_ref[rows, :] = thr
        s2_ref[rows, :] = s[1]
        e1_ref[rows, :] = jnp.exp(s[0] - v1[0]) / z
        e2_ref[rows, :] = jnp.exp(s[1] - v2[0])


def _peer_chunk(c, ec, h_ref, u_ref, v_ref, acc_ref, w_prev_ref, w_next_ref,
                thr_ref, e1_ref, s2_ref, e2_ref):
    chunks = ec // N_KEYS
    tb = h_ref.shape[0]
    tw = min(tb, 2 * LANES)
    if w_prev_ref is not None:
        v_bf = v_ref[...].astype(BF16)
    if w_next_ref is not None:
        u_bf = u_ref[...].astype(BF16)
    for tj in range(tb // tw):
        cols = slice(tj * tw, (tj + 1) * tw)
        if w_prev_ref is not None:
            acc_ref[:, cols] += lax.dot_general(v_bf, w_prev_ref[:, cols], (((0,), (0,)), ((), ())),
                                                preferred_element_type=F32)
        if w_next_ref is None:
            continue
        a = lax.dot_general(u_bf, h_ref[cols, :], (((1,), (1,)), ((), ())), preferred_element_type=F32)
        act = 0.5 * a * (1.0 + lax.erf(a * INV_SQRT2))
        for il in range(chunks):
            gate = None
            for h in range(PEER_HEADS):
                row = h * N_KEYS + c * chunks + il
                thr = thr_ref[pl.ds(row, 1), cols]
                e1 = e1_ref[pl.ds(row, 1), cols]
                keys = slice(h * N_KEYS, (h + 1) * N_KEYS)
                g = jnp.where(s2_ref[keys, cols] >= thr, e1 * e2_ref[keys, cols], 0.0)
                gate = g if gate is None else gate + g
            rows = slice(il * N_KEYS, (il + 1) * N_KEYS)
            w_next_ref[rows, cols] = (act[rows, :] * gate).astype(BF16)


def _peer_kernel(h_ref, qp_ref, sk_ref, x1_ref, u_ref, v_ref, o_ref,
                 acc_ref, wa_ref, wb_ref, thr_ref, e1_ref, s2_ref, e2_ref, *, ec, n_chunks):
    assert n_chunks % 2 == 0
    c = pl.program_id(1)
    step = functools.partial(_peer_chunk, c, ec, h_ref, u_ref, v_ref, acc_ref)
    tables = (thr_ref, e1_ref, s2_ref, e2_ref)

    @pl.when(c == 0)
    def _():
        acc_ref[...] = jnp.zeros(acc_ref.shape, F32)
        _peer_tables(qp_ref, sk_ref, *tables)
        step(None, wb_ref, *tables)

    @pl.when((c % 2 == 0) & (c > 0) & (c < n_chunks))
    def _():
        step(wa_ref, wb_ref, *tables)

    @pl.when(c % 2 == 1)
    def _():
        step(wb_ref, wa_ref, *tables)

    @pl.when(c == n_chunks)
    def _():
        step(wa_ref, None, *tables)
        o_ref[...] = x1_ref[...] + acc_ref[...].T


def peer_layer(x1, h2, qp, sub_keys, peer_u, peer_v, tb, ec):
    m, d = x1.shape
    tb = min(tb, m)
    n_chunks = N_EXPERTS // ec
    tok = lambda: pl.BlockSpec((tb, d), lambda n, c: (n, 0))
    tok_once = lambda: pl.BlockSpec((tb, d), lambda n, c: (n, 0), pipeline_mode=pl.Buffered(1))
    tab = lambda: pltpu.VMEM((PEER_HEADS * N_KEYS, tb), F32)
    wbuf = lambda: pltpu.VMEM((ec, tb), BF16)
    return pl.pallas_call(
        functools.partial(_peer_kernel, ec=ec, n_chunks=n_chunks),
        grid=(m // tb, n_chunks + 1),
        in_specs=[tok(), tok(), pl.BlockSpec((2, N_KEYS, D_KEY // 2), lambda n, c: (0, 0, 0)), tok_once(),
                  pl.BlockSpec((ec, d), lambda n, c: (jnp.minimum(c, n_chunks - 1), 0)),
                  pl.BlockSpec((ec, d), lambda n, c: (jnp.maximum(c - 1, 0), 0))],
        out_specs=tok(),
        out_shape=jax.ShapeDtypeStruct((m, d), F32),
        scratch_shapes=[pltpu.VMEM((d, tb), F32), wbuf(), wbuf(), tab(), tab(), tab(), tab()],
        compiler_params=_cparams("parallel", "arbitrary"),
        name="peer_layer",
    )(h2, qp, sub_keys, x1, peer_u, peer_v)


def _token_stack(x, rope_tabs, table_blocks, w, tiles):
    h = rmsnorm_cast(x, w["norm1_w"], tiles["norm"])
    w_in = w["w_in"]
    z_a = matmul(h, w_in, BF16, tiles["mm"], 1024, 0, KV_AT)
    z_b = matmul(h, w_in, F32, tiles["mm"], 1024, KV_AT, MEMQ_AT - KV_AT)
    z_m = matmul(h, w_in, BF16, tiles["mm"], 1024, MEMQ_AT, GATES_AT - MEMQ_AT)
    z_g = matmul(h, w_in, BF16, tiles["mm"], 1024, GATES_AT, 3 * D_MODEL)
    q, k, k_bf = qk_rope(z_a, z_b, w["q_norm_w"], w["k_norm_w"], rope_tabs, tiles["rope"], table_blocks)
    return z_a, z_b, z_m, z_g, q, k, k_bf


def _finish(x, merged, w, tiles):
    x1, h2 = out_proj(x, merged, w["w_o"], w["norm2_w"], tiles["out"])
    qp = matmul(h2, w["w_peer_q"], BF16, tiles["mm"], 1024)
    return peer_layer(x1, h2, qp, w["sub_keys"], w["peer_u"], w["peer_v"], tiles["peer_tb"], tiles["peer_ec"])


def kernel(x_prompt, x_sample, mem_prompt, cache_k, cache_v, page_table, state_conv, cache_mem_k, cache_mem_v,
           norm1_w, w_in, conv_w, q_norm_w, k_norm_w, lambda_q1, lambda_k1, lambda_q2, lambda_k2, subln_w,
           mem_norm_w, w_mem_kv, mq_norm_w, mk_norm_w, w_conv_out, w_diff_out, w_cross_out, w_o, norm2_w,
           w_peer_q, sub_keys, peer_u, peer_v):
    batch, seq, d = x_prompt.shape
    n_dec = x_sample.shape[0]
    l = 0
    w = {
        "norm1_w": norm1_w[l], "norm2_w": norm2_w[l], "q_norm_w": q_norm_w[l], "k_norm_w": k_norm_w[l],
        "w_in": w_in[l].astype(BF16),
        "w_o": w_o[l].astype(BF16), "w_peer_q": w_peer_q[l].astype(BF16), "sub_keys": sub_keys[l].astype(BF16),
        "peer_u": peer_u[l], "peer_v": peer_v[l],
    }
    w_conv_out_b = w_conv_out[l].astype(BF16)
    w_diff_out_b = w_diff_out[l].astype(BF16)
    w_cross_out_b = w_cross_out[l].astype(BF16)
    lam_vecs = tuple(v[l].reshape(1, DK) for v in (lambda_q1, lambda_k1, lambda_q2, lambda_k2))

    tiles_p = {"norm": 512, "mm": 1024, "rope": 512, "out": 512, "peer_tb": 512, "peer_ec": 512}
    xp = x_prompt.reshape(batch * seq, d)
    tabs_p = _rope_tables(jnp.arange(seq, dtype=jnp.int32))
    z_a, z_b, z_m, z_g, q, k, k_bf = _token_stack(xp, tabs_p, seq // tiles_p["rope"], w, tiles_p)
    h_mem = rmsnorm_cast(mem_prompt.reshape(batch * N_MEM, d), mem_norm_w[l], 512)
    kv_mem = matmul(h_mem, w_mem_kv[l].astype(BF16), F32, 1024, 1024)
    mk_p = group_norm(kv_mem, mk_norm_w[l], 512)
    mv_p = kv_mem[:, CROSS_W:]
    o_diff = diff_attn_prompt(q, k_bf, z_b, lam_vecs, subln_w[l], batch, seq, 256)
    o_mem = mem_attn(z_m, 0, mq_norm_w[l], mk_p, mv_p, batch, seq, 1024)
    merged = merge_branches(z_a, None, conv_w[l], o_diff, o_mem, z_g, w_conv_out_b, w_diff_out_b, w_cross_out_b,
                            256, seq)
    conv_p = conv_state_prompt(z_a, batch, seq)
    y_prompt = _finish(xp, merged, w, tiles_p).reshape(batch, seq, d)

    tiles_s = {"norm": 128, "mm": 128, "rope": 128, "out": 128, "peer_tb": 128, "peer_ec": 512}
    m_s = n_dec * SAMPLE_ROWS
    xs = jnp.pad(x_sample, ((0, 0), (0, SAMPLE_ROWS - x_sample.shape[1]), (0, 0))).reshape(m_s, d)
    tabs_s = _rope_tables(jnp.full((m_s,), PAST_LEN, jnp.int32))
    zs_a, zs_b, zs_m, zs_g, qs, ks, _ = _token_stack(xs, tabs_s, 1, w, tiles_s)
    pages = cache_k.shape[1]
    new_tok = lambda a: a.reshape(n_dec, SAMPLE_ROWS, N_HEADS, HEAD_W)[:, 0].astype(F32)
    os_heads = diff_attn_decode(new_tok(qs), new_tok(ks), new_tok(zs_b[:, QK_W:]),
                                cache_k[l].reshape(pages, PAGE_SIZE * N_HEADS, HEAD_W),
                                cache_v[l].reshape(pages, PAGE_SIZE * N_HEADS, HEAD_W),
                                page_table, lam_vecs, subln_w[l], 8)
    os_diff = jnp.pad(os_heads.reshape(n_dec, 1, QK_W).astype(BF16),
                      ((0, 0), (0, SAMPLE_ROWS - 1), (0, 0))).reshape(m_s, QK_W)
    os_mem = mem_attn(zs_m, 0, mq_norm_w[l], cache_mem_k[l].reshape(n_dec * N_MEM, CROSS_W),
                      cache_mem_v[l].reshape(n_dec * N_MEM, CROSS_W), n_dec, SAMPLE_ROWS, SAMPLE_ROWS)
    pad_state = lambda r: jnp.pad(state_conv[l][:, r:r + 1], ((0, 0), (0, SAMPLE_ROWS - 1), (0, 0))).reshape(m_s, D_CONV)
    merged_s = merge_branches(zs_a, (pad_state(0), pad_state(1)), conv_w[l], os_diff, os_mem, zs_g,
                              w_conv_out_b, w_diff_out_b, w_cross_out_b, 128, SAMPLE_ROWS)
    us = sample_u(zs_a).reshape(n_dec, SAMPLE_ROWS, D_CONV)[:, 0:1]
    conv_s = jnp.concatenate([state_conv[l][:, 1:2], us], axis=1)
    y_sample = _finish(xs, merged_s, w, tiles_s).reshape(n_dec, SAMPLE_ROWS, d)[:, 0:1]

    first = lambda a, n: a.reshape(n, SAMPLE_ROWS, N_HEADS, HEAD_W)[:, 0:1]
    return (y_prompt, y_sample,
            k.reshape(1, batch, seq, N_HEADS, HEAD_W), z_b[:, QK_W:].reshape(1, batch, seq, N_HEADS, DV),
            conv_p[None], mk_p.reshape(1, batch, N_MEM, MEM_HEADS, MEM_HD),
            mv_p.reshape(1, batch, N_MEM, MEM_HEADS, MEM_HD),
            first(ks, n_dec)[None], first(zs_b[:, QK_W:], n_dec)[None], conv_s[None])
```

```python
import functools
import math

import jax
import jax.numpy as jnp
import numpy as np
from jax import lax
from jax.experimental import pallas as pl
from jax.experimental.pallas import tpu as pltpu

F32 = jnp.float32
BF16 = jnp.bfloat16

D_MODEL = 2048
D_CONV = D_MODEL // 2
N_HEADS = 8
DK = D_MODEL // 32
DV = 2 * DK
HEAD_W = 2 * DK
QK_W = N_HEADS * 2 * DK
ROT_DIM = DK // 4
ROPE_THETA = 500000.0
N_MEM = 256
MEM_HEADS = 4
MEM_HD = D_MODEL // 8
CROSS_W = MEM_HEADS * MEM_HD
N_KEYS = 128
N_EXPERTS = N_KEYS * N_KEYS
PEER_HEADS = 8
PEER_TOPK = 16
D_KEY = 256
PAST_LEN = 16384
PAGE_SIZE = 128
EPS = 1e-6
NEG_INF = -1e30
UNREACHABLE = 3e38
KV_AT = 3 * D_CONV + QK_W
MEMQ_AT = KV_AT + 2 * QK_W
GATES_AT = MEMQ_AT + CROSS_W
LAM_INIT = 0.8 - 0.6 * math.exp(-0.3 * 0)
INV_SQRT2 = 1.0 / math.sqrt(2.0)

LANES = 128
SAMPLE_ROWS = 16
VMEM_LIMIT = 56 * 1024 * 1024


def _cparams(*sem, flags=None):
    return pltpu.CompilerParams(dimension_semantics=sem, vmem_limit_bytes=VMEM_LIMIT, flags=flags)


def _rmsnorm_kernel(x_ref, w_ref, o_ref):
    x = x_ref[...]
    ms = jnp.mean(x * x, axis=-1, keepdims=True)
    o_ref[...] = (x * lax.rsqrt(ms + EPS) * w_ref[...]).astype(o_ref.dtype)


def rmsnorm_cast(x, w, tm):
    m, d = x.shape
    return pl.pallas_call(
        _rmsnorm_kernel,
        grid=(m // tm,),
        in_specs=[pl.BlockSpec((tm, d), lambda i: (i, 0)), pl.BlockSpec((1, d), lambda i: (0, 0))],
        out_specs=pl.BlockSpec((tm, d), lambda i: (i, 0)),
        out_shape=jax.ShapeDtypeStruct((m, d), BF16),
        compiler_params=_cparams("parallel"),
        name="rmsnorm_cast",
    )(x, w.reshape(1, d))


def _matmul_kernel(a_ref, b_ref, o_ref):
    o_ref[...] = jnp.dot(a_ref[...], b_ref[...], preferred_element_type=F32).astype(o_ref.dtype)


def matmul(a, b, out_dtype, tm, tn, col0=0, n=None):
    m, k = a.shape
    n = b.shape[1] if n is None else n
    tm = min(tm, m)
    tn = min(tn, n)
    first = col0 // tn
    return pl.pallas_call(
        _matmul_kernel,
        grid=(m // tm, n // tn),
        in_specs=[pl.BlockSpec((tm, k), lambda i, j: (i, 0)), pl.BlockSpec((k, tn), lambda i, j: (0, j + first))],
        out_specs=pl.BlockSpec((tm, tn), lambda i, j: (i, j)),
        out_shape=jax.ShapeDtypeStruct((m, n), out_dtype),
        compiler_params=_cparams("parallel", "arbitrary"),
        name="matmul",
    )(a, b)


def _group_norm_kernel(x_ref, w_ref, o_ref):
    for g in range(MEM_HEADS):
        c = x_ref[:, g * MEM_HD:(g + 1) * MEM_HD]
        ms = jnp.mean(c * c, axis=-1, keepdims=True)
        o_ref[:, g * MEM_HD:(g + 1) * MEM_HD] = c * lax.rsqrt(ms + EPS) * w_ref[...]


def group_norm(x, w, tm):
    m = x.shape[0]
    return pl.pallas_call(
        _group_norm_kernel,
        grid=(m // tm,),
        in_specs=[pl.BlockSpec((tm, CROSS_W), lambda i: (i, 0)), pl.BlockSpec((1, MEM_HD), lambda i: (0, 0))],
        out_specs=pl.BlockSpec((tm, CROSS_W), lambda i: (i, 0)),
        out_shape=jax.ShapeDtypeStruct((m, CROSS_W), F32),
        compiler_params=_cparams("parallel"),
        name="group_norm",
    )(x, w.reshape(1, MEM_HD))


def _sub_head_norm_rope(x, w, ones_bd, ca, cm, cp):
    x2 = x * x
    hi = x2.astype(BF16)
    lo = (x2 - hi.astype(F32)).astype(BF16)
    ss = jnp.dot(hi, ones_bd, preferred_element_type=F32) + jnp.dot(lo, ones_bd, preferred_element_type=F32)
    y = x * lax.rsqrt(ss * (1.0 / DK) + EPS) * w
    half = ROT_DIM // 2
    y_up = pltpu.roll(y, LANES - half, axis=1)
    y_dn = pltpu.roll(y, half, axis=1)
    return y * ca + y_up * cm + y_dn * cp


def _qk_rope_kernel(zq_ref, zk_ref, qw_ref, kw_ref, bd_ref, ca_ref, cm_ref, cp_ref, q_ref, k_ref, kb_ref):
    ones_bd = bd_ref[...]
    ca, cm, cp = ca_ref[...], cm_ref[...], cp_ref[...]
    for c in range(QK_W // LANES):
        sl = slice(c * LANES, (c + 1) * LANES)
        q = _sub_head_norm_rope(zq_ref[:, sl].astype(F32), qw_ref[...], ones_bd, ca, cm, cp)
        q_ref[:, sl] = (q * (DK ** -0.5)).astype(BF16)
        k = _sub_head_norm_rope(zk_ref[:, sl], kw_ref[...], ones_bd, ca, cm, cp)
        k_ref[:, sl] = k
        kb_ref[:, sl] = k.astype(BF16)


def _rope_tables(pos):
    half = ROT_DIM // 2
    inv_freq = ROPE_THETA ** (-jnp.arange(half, dtype=F32) / half)
    ang = pos.astype(F32)[:, None] * inv_freq[None, :]
    cos, sin = jnp.cos(ang), jnp.sin(ang)
    t = pos.shape[0]
    ones = jnp.ones((t, DK - ROT_DIM), F32)
    zeros = jnp.zeros((t, DK - ROT_DIM), F32)
    zh = jnp.zeros((t, half), F32)
    ca = jnp.concatenate([cos, cos, ones], axis=-1)
    cm = jnp.concatenate([-sin, zh, zeros], axis=-1)
    cp = jnp.concatenate([zh, sin, zeros], axis=-1)
    return tuple(jnp.concatenate([a, a], axis=-1) for a in (ca, cm, cp))


def qk_rope(z_a, z_b, q_norm_w, k_norm_w, tables, tm, table_blocks):
    m = z_a.shape[0]
    qw = jnp.tile(q_norm_w.reshape(1, DK), (1, 2))
    kw = jnp.tile(k_norm_w.reshape(1, DK), (1, 2))
    grp = np.arange(LANES) // DK
    ones_bd = jnp.asarray(grp[:, None] == grp[None, :], BF16)
    row_spec = lambda col: pl.BlockSpec((tm, QK_W), lambda i: (i, col))
    tab_spec = pl.BlockSpec((tm, LANES), lambda i: (i % table_blocks, 0))
    vec_spec = pl.BlockSpec((1, LANES), lambda i: (0, 0))
    return pl.pallas_call(
        _qk_rope_kernel,
        grid=(m // tm,),
        in_specs=[row_spec(3), row_spec(0), vec_spec, vec_spec,
                  pl.BlockSpec((LANES, LANES), lambda i: (0, 0)), tab_spec, tab_spec, tab_spec],
        out_specs=[row_spec(0), row_spec(0), row_spec(0)],
        out_shape=[jax.ShapeDtypeStruct((m, QK_W), BF16), jax.ShapeDtypeStruct((m, QK_W), F32),
                   jax.ShapeDtypeStruct((m, QK_W), BF16)],
        compiler_params=_cparams("parallel"),
        name="qk_rope",
    )(z_a, z_b, qw, kw, ones_bd, *tables)


def _lambda_value(lq1_ref, lk1_ref, lq2_ref, lk2_ref):
    a = jnp.sum(lq1_ref[...] * lk1_ref[...], axis=-1, keepdims=True)
    b = jnp.sum(lq2_ref[...] * lk2_ref[...], axis=-1, keepdims=True)
    return jnp.exp(a) - jnp.exp(b) + LAM_INIT


def _sub_layer_norm(o, sw):
    ms = jnp.mean(o * o, axis=-1, keepdims=True)
    return o * lax.rsqrt(ms + EPS) * sw * (1.0 - LAM_INIT)


def _diff_attn_kernel(lq1_ref, lk1_ref, lq2_ref, lk2_ref, sw_ref, q_ref, k_ref, v_ref, o_ref, *, seq, tq):
    lam = _lambda_value(lq1_ref, lk1_ref, lq2_ref, lk2_ref)
    sw = sw_ref[...]
    for i in range(seq // tq):
        n_k = (i + 1) * tq
        q = q_ref[i * tq:(i + 1) * tq, :]
        k = k_ref[0:n_k, :]
        v = v_ref[0:n_k, :].astype(BF16)
        row = lax.broadcasted_iota(jnp.int32, (tq, n_k), 0) + i * tq
        col = lax.broadcasted_iota(jnp.int32, (tq, n_k), 1)
        visible = col <= row

        def softmax_parts(qc, kc):
            s = lax.dot_general(qc, kc, (((1,), (1,)), ((), ())), preferred_element_type=F32)
            s = jnp.where(visible, s, NEG_INF)
            e = jnp.exp(s - jnp.max(s, axis=-1, keepdims=True))
            return e, jnp.sum(e, axis=-1, keepdims=True)

        e1, l1 = softmax_parts(q[:, :DK], k[:, :DK])
        e2, l2 = softmax_parts(q[:, DK:], k[:, DK:])
        a = e1 * (1.0 / l1) - e2 * (lam / l2)
        o = jnp.dot(a.astype(BF16), v, preferred_element_type=F32)
        o_ref[i * tq:(i + 1) * tq, :] = _sub_layer_norm(o, sw).astype(o_ref.dtype)


def diff_attn_prompt(q, k, z_b, lam_vecs, subln_w, batch, seq, tq):
    vec = pl.BlockSpec((1, DK), lambda b, h: (0, 0))
    head = lambda off: pl.BlockSpec((seq, HEAD_W), lambda b, h: (b, h + off))
    return pl.pallas_call(
        functools.partial(_diff_attn_kernel, seq=seq, tq=tq),
        grid=(batch, N_HEADS),
        in_specs=[vec, vec, vec, vec, pl.BlockSpec((1, DV), lambda b, h: (0, 0)),
                  head(0), head(0), head(N_HEADS)],
        out_specs=head(0),
        out_shape=jax.ShapeDtypeStruct((batch * seq, QK_W), BF16),
        compiler_params=_cparams("parallel", "parallel"),
        name="diff_attn_prompt",
    )(*lam_vecs, subln_w.reshape(1, DV), q, k, z_b)


def _diff_attn_decode_kernel(pt_ref, lq1_ref, lk1_ref, lq2_ref, lk2_ref, sw_ref, q_ref, kn_ref, vn_ref, *rest,
                             pages_per_step):
    del pt_ref
    k_refs = rest[:pages_per_step]
    v_refs = rest[pages_per_step:2 * pages_per_step]
    o_ref, qs_ref, m_ref, l_ref, acc_ref = rest[2 * pages_per_step:]
    j = pl.program_id(1)
    page_rows = PAGE_SIZE * N_HEADS

    @pl.when(j == 0)
    def _():
        q = q_ref[...]
        lane = lax.broadcasted_iota(jnp.int32, q.shape, 1)
        qrows = jnp.concatenate([jnp.where(lane < DK, q, 0.0), jnp.where(lane >= DK, q, 0.0)], axis=0)
        qs_ref[...] = qrows.astype(BF16)
        k_new = kn_ref[...].astype(BF16).astype(F32)
        s_new = jnp.sum(qrows * jnp.concatenate([k_new, k_new], axis=0), axis=-1, keepdims=True)
        m_ref[...] = jnp.broadcast_to(s_new, m_ref.shape)
        l_ref[...] = jnp.ones(l_ref.shape, F32)
        acc_ref[...] = jnp.concatenate([vn_ref[...], vn_ref[...]], axis=0)

    qs = qs_ref[...]
    own = (lax.broadcasted_iota(jnp.int32, (2 * N_HEADS, page_rows), 1) % N_HEADS
           == lax.broadcasted_iota(jnp.int32, (2 * N_HEADS, page_rows), 0) % N_HEADS)
    s = jnp.concatenate(
        [jnp.where(own, lax.dot_general(qs, k_refs[r][...].astype(BF16), (((1,), (1,)), ((), ())),
                                        preferred_element_type=F32), NEG_INF)
         for r in range(pages_per_step)], axis=-1)
    m_old = m_ref[:, 0:1]
    m_new = jnp.maximum(m_old, jnp.max(s, axis=-1, keepdims=True))
    alpha = jnp.exp(m_old - m_new)
    p = jnp.exp(s - m_new)
    l_ref[...] = jnp.broadcast_to(alpha * l_ref[:, 0:1] + jnp.sum(p, axis=-1, keepdims=True), l_ref.shape)
    m_ref[...] = jnp.broadcast_to(m_new, m_ref.shape)
    pv = acc_ref[...] * alpha
    for r in range(pages_per_step):
        pv = pv + jnp.dot(p[:, r * page_rows:(r + 1) * page_rows].astype(BF16), v_refs[r][...].astype(BF16),
                          preferred_element_type=F32)
    acc_ref[...] = pv

    @pl.when(j == pl.num_programs(1) - 1)
    def _():
        lam = _lambda_value(lq1_ref, lk1_ref, lq2_ref, lk2_ref)
        o = acc_ref[...] / l_ref[:, 0:1]
        od = o[0:N_HEADS, :] - lam * o[N_HEADS:2 * N_HEADS, :]
        o_ref[...] = _sub_layer_norm(od, sw_ref[...])


def diff_attn_decode(q, k_new, v_new, cache_k, cache_v, page_table, lam_vecs, subln_w, pages_per_step):
    n_batch, n_pages = page_table.shape
    vec = pl.BlockSpec((1, DK), lambda b, j, pt: (0, 0))
    tok = lambda: pl.BlockSpec((None, N_HEADS, HEAD_W), lambda b, j, pt: (b, 0, 0))

    def page_spec(r):
        return pl.BlockSpec((None, PAGE_SIZE * N_HEADS, HEAD_W),
                            lambda b, j, pt: (pt[b, j * pages_per_step + r], 0, 0))

    grid_spec = pltpu.PrefetchScalarGridSpec(
        num_scalar_prefetch=1,
        grid=(n_batch, n_pages // pages_per_step),
        in_specs=[vec, vec, vec, vec, pl.BlockSpec((1, DV), lambda b, j, pt: (0, 0)), tok(), tok(), tok()]
        + [page_spec(r) for r in range(pages_per_step)] + [page_spec(r) for r in range(pages_per_step)],
        out_specs=tok(),
        scratch_shapes=[pltpu.VMEM((2 * N_HEADS, HEAD_W), BF16), pltpu.VMEM((2 * N_HEADS, LANES), F32),
                        pltpu.VMEM((2 * N_HEADS, LANES), F32), pltpu.VMEM((2 * N_HEADS, HEAD_W), F32)],
    )
    return pl.pallas_call(
        functools.partial(_diff_attn_decode_kernel, pages_per_step=pages_per_step),
        grid_spec=grid_spec,
        out_shape=jax.ShapeDtypeStruct((n_batch, N_HEADS, HEAD_W), F32),
        compiler_params=_cparams("parallel", "arbitrary"),
        name="diff_attn_decode",
    )(page_table, *lam_vecs, subln_w.reshape(1, DV), q, k_new, v_new,
      *([cache_k] * pages_per_step), *([cache_v] * pages_per_step))


def _mem_attn_kernel(q_ref, w_ref, mk_ref, mv_ref, o_ref):
    q = q_ref[...].astype(F32)
    ms = jnp.mean(q * q, axis=-1, keepdims=True)
    qn = (q * lax.rsqrt(ms + EPS) * w_ref[...] * (MEM_HD ** -0.5)).astype(BF16)
    s = lax.dot_general(qn, mk_ref[...].astype(BF16), (((1,), (1,)), ((), ())), preferred_element_type=F32)
    e = jnp.exp(s - jnp.max(s, axis=-1, keepdims=True))
    p = e * (1.0 / jnp.sum(e, axis=-1, keepdims=True))
    o_ref[...] = jnp.dot(p.astype(BF16), mv_ref[...].astype(BF16), preferred_element_type=F32).astype(o_ref.dtype)


def mem_attn(z_q, q_col, mq_norm_w, mk, mv, batch, seq, tq):
    nq = seq // tq
    return pl.pallas_call(
        _mem_attn_kernel,
        grid=(batch, MEM_HEADS, nq),
        in_specs=[pl.BlockSpec((tq, MEM_HD), lambda b, h, i: (b * nq + i, q_col + h)),
                  pl.BlockSpec((1, MEM_HD), lambda b, h, i: (0, 0)),
                  pl.BlockSpec((N_MEM, MEM_HD), lambda b, h, i: (b, h)),
                  pl.BlockSpec((N_MEM, MEM_HD), lambda b, h, i: (b, h))],
        out_specs=pl.BlockSpec((tq, MEM_HD), lambda b, h, i: (b * nq + i, h)),
        out_shape=jax.ShapeDtypeStruct((batch * seq, CROSS_W), BF16),
        compiler_params=_cparams("parallel", "parallel", "parallel"),
        name="mem_attn",
    )(z_q, mq_norm_w.reshape(1, MEM_HD), mk, mv)


def _merge_tail(zb, conv, od_ref, om_ref, g0_ref, g1_ref, g2_ref, wc_ref, wd_ref, wx_ref, o_ref):
    yc = jnp.dot((zb * conv).astype(BF16), wc_ref[...], preferred_element_type=F32)
    yd = jnp.dot(od_ref[...], wd_ref[...], preferred_element_type=F32)
    ym = jnp.dot(om_ref[...], wx_ref[...], preferred_element_type=F32)
    merged = (jax.nn.sigmoid(g0_ref[...].astype(F32)) * yc + jax.nn.sigmoid(g1_ref[...].astype(F32)) * yd
              + jax.nn.sigmoid(g2_ref[...].astype(F32)) * ym)
    o_ref[...] = merged.astype(o_ref.dtype)


def _merge_prompt_kernel(zb_ref, zc_ref, zx_ref, hc_ref, hx_ref, cw_ref, *rest, tiles_per_seq, tm):
    u = zc_ref[...].astype(F32) * zx_ref[...].astype(F32)
    first = pl.program_id(0) % tiles_per_seq == 0
    halo = jnp.where(first, 0.0, hc_ref[...].astype(F32) * hx_ref[...].astype(F32))
    h1 = halo[SAMPLE_ROWS - 1:SAMPLE_ROWS, :]
    h2 = halo[SAMPLE_ROWS - 2:SAMPLE_ROWS - 1, :]
    row = lax.broadcasted_iota(jnp.int32, u.shape, 0)
    u1 = jnp.where(row == 0, h1, pltpu.roll(u, 1, axis=0))
    u2 = jnp.where(row == 0, h2, jnp.where(row == 1, h1, pltpu.roll(u, 2, axis=0)))
    conv = cw_ref[0:1, :] * u2 + cw_ref[1:2, :] * u1 + cw_ref[2:3, :] * u
    _merge_tail(zb_ref[...].astype(F32), conv, *rest)


def _merge_sample_kernel(zb_ref, zc_ref, zx_ref, p2_ref, p1_ref, cw_ref, *rest):
    u = zc_ref[...].astype(F32) * zx_ref[...].astype(F32)
    conv = cw_ref[0:1, :] * p2_ref[...] + cw_ref[1:2, :] * p1_ref[...] + cw_ref[2:3, :] * u
    _merge_tail(zb_ref[...].astype(F32), conv, *rest)


def merge_branches(z_a, prev, conv_w, o_diff, o_mem, z_g, w_conv_out, w_diff_out, w_cross_out, tm, seq):
    m = z_a.shape[0]
    col = lambda c, w: pl.BlockSpec((tm, w), lambda i: (i, c))
    const = lambda shape: pl.BlockSpec(shape, lambda i: (0, 0))
    tail_specs = [col(0, D_CONV), col(0, CROSS_W), col(0, D_MODEL), col(1, D_MODEL), col(2, D_MODEL),
                  const((D_CONV, D_MODEL)), const((QK_W, D_MODEL)), const((CROSS_W, D_MODEL))]
    tail_args = (o_diff, o_mem, z_g, z_g, z_g, w_conv_out, w_diff_out, w_cross_out)
    if prev is None:
        blocks = tm // SAMPLE_ROWS
        halo = lambda c: pl.BlockSpec((SAMPLE_ROWS, D_CONV), lambda i: (jnp.maximum(i * blocks - 1, 0), c))
        kern = functools.partial(_merge_prompt_kernel, tiles_per_seq=seq // tm, tm=tm)
        head_specs = [col(0, D_CONV), col(1, D_CONV), col(2, D_CONV), halo(1), halo(2), const((3, D_CONV))]
        head_args = (z_a, z_a, z_a, z_a, z_a, conv_w)
    else:
        kern = _merge_sample_kernel
        head_specs = [col(0, D_CONV), col(1, D_CONV), col(2, D_CONV), col(0, D_CONV), col(0, D_CONV),
                      const((3, D_CONV))]
        head_args = (z_a, z_a, z_a, prev[0], prev[1], conv_w)
    return pl.pallas_call(
        kern,
        grid=(m // tm,),
        in_specs=head_specs + tail_specs,
        out_specs=col(0, D_MODEL),
        out_shape=jax.ShapeDtypeStruct((m, D_MODEL), BF16),
        compiler_params=_cparams("parallel"),
        name="merge_branches",
    )(*head_args, *tail_args)


def _conv_state_kernel(zc_ref, zx_ref, o_ref):
    u = zc_ref[...].astype(F32) * zx_ref[...].astype(F32)
    o_ref[...] = u[SAMPLE_ROWS - 2:SAMPLE_ROWS, :]


def conv_state_prompt(z_a, batch, seq):
    blocks = seq // SAMPLE_ROWS
    tail = lambda c: pl.BlockSpec((SAMPLE_ROWS, D_CONV), lambda b: (b * blocks + blocks - 1, c))
    return pl.pallas_call(
        _conv_state_kernel,
        grid=(batch,),
        in_specs=[tail(1), tail(2)],
        out_specs=pl.BlockSpec((None, 2, D_CONV), lambda b: (b, 0, 0)),
        out_shape=jax.ShapeDtypeStruct((batch, 2, D_CONV), F32),
        compiler_params=_cparams("parallel"),
        name="conv_state_prompt",
    )(z_a, z_a)


def _sample_u_kernel(zc_ref, zx_ref, o_ref):
    o_ref[...] = zc_ref[...].astype(F32) * zx_ref[...].astype(F32)


def sample_u(z_a):
    m = z_a.shape[0]
    blk = lambda c: pl.BlockSpec((m, D_CONV), lambda i: (0, c))
    return pl.pallas_call(
        _sample_u_kernel,
        grid=(1,),
        in_specs=[blk(1), blk(2)],
        out_specs=blk(0),
        out_shape=jax.ShapeDtypeStruct((m, D_CONV), F32),
        compiler_params=_cparams("arbitrary"),
        name="sample_u",
    )(z_a, z_a)


def _out_proj_kernel(x_ref, a_ref, w_ref, nw_ref, x1_ref, h_ref):
    x1 = x_ref[...] + jnp.dot(a_ref[...], w_ref[...], preferred_element_type=F32)
    x1_ref[...] = x1
    ms = jnp.mean(x1 * x1, axis=-1, keepdims=True)
    h_ref[...] = (x1 * lax.rsqrt(ms + EPS) * nw_ref[...]).astype(h_ref.dtype)


def out_proj(x, merged, w_o, norm2_w, tm):
    m, d = x.shape
    tm = min(tm, m)
    row = pl.BlockSpec((tm, d), lambda i: (i, 0))
    return pl.pallas_call(
        _out_proj_kernel,
        grid=(m // tm,),
        in_specs=[row, row, pl.BlockSpec((d, d), lambda i: (0, 0)), pl.BlockSpec((1, d), lambda i: (0, 0))],
        out_specs=[row, row],
        out_shape=[jax.ShapeDtypeStruct((m, d), F32), jax.ShapeDtypeStruct((m, d), BF16)],
        compiler_params=_cparams("parallel"),
        name="out_proj",
    )(x, merged, w_o, norm2_w.reshape(1, d))


def _top_values(x, count):
    vals = []
    for _ in range(count):
        m = jnp.max(x, axis=0, keepdims=True)
        vals.append(m)
        x = jnp.where(x == m, NEG_INF, x)
    return vals


def _peer_tables(qp_ref, sk_ref, thr_ref, e1_ref, s2_ref, e2_ref):
    for h in range(PEER_HEADS):
        s = []
        for c in range(2):
            qc = qp_ref[:, (2 * h + c) * N_KEYS:(2 * h + c + 1) * N_KEYS]
            s.append(lax.dot_general(sk_ref[c], qc, (((1,), (1,)), ((), ())), preferred_element_type=F32))
        v1 = _top_values(s[0], PEER_TOPK)
        v2 = _top_values(s[1], PEER_TOPK)
        cands = []
        for a in range(PEER_TOPK):
            for b in range(PEER_TOPK // (a + 1)):
                cands.append(v1[a] + v2[b])
        cand = jnp.concatenate(cands, axis=0)
        top = _top_values(cand, PEER_TOPK)
        smax = v1[0] + v2[0]
        z = jnp.zeros_like(smax)
        for r in range(PEER_TOPK):
            z = z + jnp.exp(top[r] - smax)
        t = top[PEER_TOPK - 1]
        thr = jnp.full(s[0].shape, UNREACHABLE, F32)
        for b in range(PEER_TOPK):
            thr = jnp.where(s[0] + v2[b] >= t, v2[b], thr)
        rows = slice(h * N_KEYS, (h + 1) * N_KEYS)
        thr_ref[rows, :] = thr
        s2_ref[rows, :] = s[1]
        e1_ref[rows, :] = jnp.exp(s[0] - v1[0]) / z
        e2_ref[rows, :] = jnp.exp(s[1] - v2[0])


def _peer_chunk(c, ec, h_ref, u_ref, v_ref, acc_ref, w_prev_ref, w_next_ref,
                thr_ref, e1_ref, s2_ref, e2_ref):
    chunks = ec // N_KEYS
    tb = h_ref.shape[0]
    tw = min(tb, 2 * LANES)
    if w_prev_ref is not None:
        v_bf = v_ref[...].astype(BF16)
    if w_next_ref is not None:
        u_bf = u_ref[...].astype(BF16)
    d_blk = acc_ref.shape[0] // chunks
    for tj in range(tb // tw):
        cols = slice(tj * tw, (tj + 1) * tw)
        if w_next_ref is not None:
            a = lax.dot_general(u_bf, h_ref[cols, :], (((1,), (1,)), ((), ())), preferred_element_type=F32)
            act = 0.5 * a * (1.0 + lax.erf(a * INV_SQRT2))
        zero_row = None
        for il in range(chunks):
            if w_prev_ref is not None:
                w_dep = w_prev_ref[:, cols]
                if zero_row is not None:
                    w_dep = w_dep + jnp.broadcast_to(zero_row, w_dep.shape)
                drows = slice(il * d_blk, (il + 1) * d_blk)
                acc_ref[drows, cols] += lax.dot_general(v_bf[:, drows], w_dep, (((0,), (0,)), ((), ())),
                                                        preferred_element_type=F32)
            if w_next_ref is None:
                continue
            gate = None
            for h in range(PEER_HEADS):
                row = h * N_KEYS + c * chunks + il
                thr = thr_ref[pl.ds(row, 1), cols]
                e1 = e1_ref[pl.ds(row, 1), cols]
                keys = slice(h * N_KEYS, (h + 1) * N_KEYS)
                g = jnp.where(s2_ref[keys, cols] >= thr, e1 * e2_ref[keys, cols], 0.0)
                gate = g if gate is None else gate + g
            rows = slice(il * N_KEYS, (il + 1) * N_KEYS)
            w_next_ref[rows, cols] = (act[rows, :] * gate).astype(BF16)
            bits = pltpu.bitcast(gate[0:8, :], jnp.uint32)
            zero_row = ((bits >> 16) >> 16)[0:1, :].astype(F32).astype(BF16)


def _peer_kernel(h_ref, qp_ref, sk_ref, x1_ref, u_ref, v_ref, o_ref,
                 acc_ref, wa_ref, wb_ref, thr_ref, e1_ref, s2_ref, e2_ref, *, ec, n_chunks):
    assert n_chunks % 2 == 0
    c = pl.program_id(1)
    step = functools.partial(_peer_chunk, c, ec, h_ref, u_ref, v_ref, acc_ref)
    tables = (thr_ref, e1_ref, s2_ref, e2_ref)

    @pl.when(c == 0)
    def _():
        acc_ref[...] = jnp.zeros(acc_ref.shape, F32)
        _peer_tables(qp_ref, sk_ref, *tables)
        step(None, wb_ref, *tables)

    @pl.when((c % 2 == 0) & (c > 0) & (c < n_chunks))
    def _():
        step(wa_ref, wb_ref, *tables)

    @pl.when(c % 2 == 1)
    def _():
        step(wb_ref, wa_ref, *tables)

    @pl.when(c == n_chunks)
    def _():
        step(wa_ref, None, *tables)
        o_ref[...] = x1_ref[...] + acc_ref[...].T


def peer_layer(x1, h2, qp, sub_keys, peer_u, peer_v, tb, ec):
    m, d = x1.shape
    tb = min(tb, m)
    n_chunks = N_EXPERTS // ec
    tok = lambda: pl.BlockSpec((tb, d), lambda n, c: (n, 0))
    tok_once = lambda: pl.BlockSpec((tb, d), lambda n, c: (n, 0), pipeline_mode=pl.Buffered(1))
    tab = lambda: pltpu.VMEM((PEER_HEADS * N_KEYS, tb), F32)
    wbuf = lambda: pltpu.VMEM((ec, tb), BF16)
    return pl.pallas_call(
        functools.partial(_peer_kernel, ec=ec, n_chunks=n_chunks),
        grid=(m // tb, n_chunks + 1),
        in_specs=[tok(), tok(), pl.BlockSpec((2, N_KEYS, D_KEY // 2), lambda n, c: (0, 0, 0)), tok_once(),
                  pl.BlockSpec((ec, d), lambda n, c: (jnp.minimum(c, n_chunks - 1), 0)),
                  pl.BlockSpec((ec, d), lambda n, c: (jnp.maximum(c - 1, 0), 0))],
        out_specs=tok(),
        out_shape=jax.ShapeDtypeStruct((m, d), F32),
        scratch_shapes=[pltpu.VMEM((d, tb), F32), wbuf(), wbuf(), tab(), tab(), tab(), tab()],
        compiler_params=_cparams("parallel", "arbitrary"),
        name="peer_layer",
    )(h2, qp, sub_keys, x1, peer_u, peer_v)


def _token_stack(x, rope_tabs, table_blocks, w, tiles):
    h = rmsnorm_cast(x, w["norm1_w"], tiles["norm"])
    w_in = w["w_in"]
    z_a = matmul(h, w_in, BF16, tiles["mm"], 1024, 0, KV_AT)
    z_b = matmul(h, w_in, F32, tiles["mm"], 1024, KV_AT, MEMQ_AT - KV_AT)
    z_m = matmul(h, w_in, BF16, tiles["mm"], 1024, MEMQ_AT, GATES_AT - MEMQ_AT)
    z_g = matmul(h, w_in, BF16, tiles["mm"], 1024, GATES_AT, 3 * D_MODEL)
    q, k, k_bf = qk_rope(z_a, z_b, w["q_norm_w"], w["k_norm_w"], rope_tabs, tiles["rope"], table_blocks)
    return z_a, z_b, z_m, z_g, q, k, k_bf


def _finish(x, merged, w, tiles):
    x1, h2 = out_proj(x, merged, w["w_o"], w["norm2_w"], tiles["out"])
    qp = matmul(h2, w["w_peer_q"], BF16, tiles["mm"], 1024)
    return peer_layer(x1, h2, qp, w["sub_keys"], w["peer_u"], w["peer_v"], tiles["peer_tb"], tiles["peer_ec"])


def kernel(x_prompt, x_sample, mem_prompt, cache_k, cache_v, page_table, state_conv, cache_mem_k, cache_mem_v,
           norm1_w, w_in, conv_w, q_norm_w, k_norm_w, lambda_q1, lambda_k1, lambda_q2, lambda_k2, subln_w,
           mem_norm_w, w_mem_kv, mq_norm_w, mk_norm_w, w_conv_out, w_diff_out, w_cross_out, w_o, norm2_w,
           w_peer_q, sub_keys, peer_u, peer_v):
    batch, seq, d = x_prompt.shape
    n_dec = x_sample.shape[0]
    l = 0
    w = {
        "norm1_w": norm1_w[l], "norm2_w": norm2_w[l], "q_norm_w": q_norm_w[l], "k_norm_w": k_norm_w[l],
        "w_in": w_in[l].astype(BF16),
        "w_o": w_o[l].astype(BF16), "w_peer_q": w_peer_q[l].astype(BF16), "sub_keys": sub_keys[l].astype(BF16),
        "peer_u": peer_u[l], "peer_v": peer_v[l],
    }
    w_conv_out_b = w_conv_out[l].astype(BF16)
    w_diff_out_b = w_diff_out[l].astype(BF16)
    w_cross_out_b = w_cross_out[l].astype(BF16)
    lam_vecs = tuple(v[l].reshape(1, DK) for v in (lambda_q1, lambda_k1, lambda_q2, lambda_k2))

    tiles_p = {"norm": 512, "mm": 1024, "rope": 512, "out": 512, "peer_tb": 512, "peer_ec": 512}
    xp = x_prompt.reshape(batch * seq, d)
    tabs_p = _rope_tables(jnp.arange(seq, dtype=jnp.int32))
    z_a, z_b, z_m, z_g, q, k, k_bf = _token_stack(xp, tabs_p, seq // tiles_p["rope"], w, tiles_p)
    h_mem = rmsnorm_cast(mem_prompt.reshape(batch * N_MEM, d), mem_norm_w[l], 512)
    kv_mem = matmul(h_mem, w_mem_kv[l].astype(BF16), F32, 1024, 1024)
    mk_p = group_norm(kv_mem, mk_norm_w[l], 512)
    mv_p = kv_mem[:, CROSS_W:]
    o_diff = diff_attn_prompt(q, k_bf, z_b, lam_vecs, subln_w[l], batch, seq, 256)
    o_mem = mem_attn(z_m, 0, mq_norm_w[l], mk_p, mv_p, batch, seq, 1024)
    merged = merge_branches(z_a, None, conv_w[l], o_diff, o_mem, z_g, w_conv_out_b, w_diff_out_b, w_cross_out_b,
                            256, seq)
    conv_p = conv_state_prompt(z_a, batch, seq)
    y_prompt = _finish(xp, merged, w, tiles_p).reshape(batch, seq, d)

    tiles_s = {"norm": 128, "mm": 128, "rope": 128, "out": 128, "peer_tb": 128, "peer_ec": 512}
    m_s = n_dec * SAMPLE_ROWS
    xs = jnp.pad(x_sample, ((0, 0), (0, SAMPLE_ROWS - x_sample.shape[1]), (0, 0))).reshape(m_s, d)
    tabs_s = _rope_tables(jnp.full((m_s,), PAST_LEN, jnp.int32))
    zs_a, zs_b, zs_m, zs_g, qs, ks, _ = _token_stack(xs, tabs_s, 1, w, tiles_s)
    pages = cache_k.shape[1]
    new_tok = lambda a: a.reshape(n_dec, SAMPLE_ROWS, N_HEADS, HEAD_W)[:, 0].astype(F32)
    os_heads = diff_attn_decode(new_tok(qs), new_tok(ks), new_tok(zs_b[:, QK_W:]),
                                cache_k[l].reshape(pages, PAGE_SIZE * N_HEADS, HEAD_W),
                                cache_v[l].reshape(pages, PAGE_SIZE * N_HEADS, HEAD_W),
                                page_table, lam_vecs, subln_w[l], 8)
    os_diff = jnp.pad(os_heads.reshape(n_dec, 1, QK_W).astype(BF16),
                      ((0, 0), (0, SAMPLE_ROWS - 1), (0, 0))).reshape(m_s, QK_W)
    os_mem = mem_attn(zs_m, 0, mq_norm_w[l], cache_mem_k[l].reshape(n_dec * N_MEM, CROSS_W),
                      cache_mem_v[l].reshape(n_dec * N_MEM, CROSS_W), n_dec, SAMPLE_ROWS, SAMPLE_ROWS)
    pad_state = lambda r: jnp.pad(state_conv[l][:, r:r + 1], ((0, 0), (0, SAMPLE_ROWS - 1), (0, 0))).reshape(m_s, D_CONV)
    merged_s = merge_branches(zs_a, (pad_state(0), pad_state(1)), conv_w[l], os_diff, os_mem, zs_g,
                              w_conv_out_b, w_diff_out_b, w_cross_out_b, 128, SAMPLE_ROWS)
    us = sample_u(zs_a).reshape(n_dec, SAMPLE_ROWS, D_CONV)[:, 0:1]
    conv_s = jnp.concatenate([state_conv[l][:, 1:2], us], axis=1)
    y_sample = _finish(xs, merged_s, w, tiles_s).reshape(n_dec, SAMPLE_ROWS, d)[:, 0:1]

    first = lambda a, n: a.reshape(n, SAMPLE_ROWS, N_HEADS, HEAD_W)[:, 0:1]
    return (y_prompt, y_sample,
            k.reshape(1, batch, seq, N_HEADS, HEAD_W), z_b[:, QK_W:].reshape(1, batch, seq, N_HEADS, DV),
            conv_p[None], mk_p.reshape(1, batch, N_MEM, MEM_HEADS, MEM_HD),
            mv_p.reshape(1, batch, N_MEM, MEM_HEADS, MEM_HD),
            first(ks, n_dec)[None], first(zs_b[:, QK_W:], n_dec)[None], conv_s[None])
```
